```python
import math
import jax, jax.numpy as jnp
from jax import lax
from jax.lax import linalg as lax_linalg
import numpy as np

D_MODEL = 2048
BATCH = 2
SEQ = 4096
DEPTH = 2

EPS = 1e-6
D_FF = 256 * ((8 * D_MODEL // 3 + 255) // 256)

GDN_HEAD_DIM = 128
GDN_HEADS = (D_MODEL // 2) // GDN_HEAD_DIM
GDN_WIDTH = GDN_HEADS * GDN_HEAD_DIM
GDN_CONV = 4
GDN_CHUNK = 64

MOBA_HEAD_DIM = 128
MOBA_HEADS = (D_MODEL // 4) // MOBA_HEAD_DIM
MOBA_WIDTH = MOBA_HEADS * MOBA_HEAD_DIM
MOBA_BLOCK = 256
MOBA_TOPK = 3
MOBA_QCHUNK = 32

SCONV_WIDTH = D_MODEL - GDN_WIDTH - MOBA_WIDTH
SCONV_K = 3

REL_BUCKETS = 32
REL_MAX_DIST = 128
NEG_INF = -1e30

PROJ_SIZES = (GDN_WIDTH,) * 4 + (GDN_HEADS,) * 2 + (MOBA_WIDTH,) * 3 + (SCONV_WIDTH,) * 3
PROJ_WIDTH = sum(PROJ_SIZES)

kernel_name = "hybrid_gdn_moba_shortconv_macaron"


def rmsnorm(x, w):
    xf = x.astype(jnp.float32)
    y = xf * lax.rsqrt(jnp.mean(xf * xf, axis=-1, keepdims=True) + EPS)
    return (y * w.astype(jnp.float32)).astype(x.dtype)


def l2norm(x):
    xf = x.astype(jnp.float32)
    return xf * lax.rsqrt(jnp.sum(xf * xf, axis=-1, keepdims=True) + EPS)


def swiglu(h, w_gate, w_up, w_down):
    return (jax.nn.silu(h @ w_gate) * (h @ w_up)) @ w_down


def split_cols(p, sizes):
    outs, start = [], 0
    for s in sizes:
        outs.append(p[..., start:start + s])
        start += s
    return outs


def causal_depthwise_conv(x, w):
    width, ch = w.shape
    return lax.conv_general_dilated(
        x, w[:, None, :].astype(x.dtype), window_strides=(1,), padding=[(width - 1, 0)],
        dimension_numbers=("NWC", "WIO", "NWC"), feature_group_count=ch)


def rel_bucket(dist):
    n = jnp.maximum(dist, 0)
    max_exact = REL_BUCKETS // 2
    nf = jnp.maximum(n, max_exact).astype(jnp.float32)
    large = max_exact + (jnp.log(nf / max_exact) / math.log(REL_MAX_DIST / max_exact)
                         * (REL_BUCKETS - max_exact)).astype(jnp.int32)
    large = jnp.minimum(large, REL_BUCKETS - 1)
    return jnp.where(n < max_exact, n, large)


def gated_delta_rule(q, k, v, g, beta):
    bsz, seq, heads, dk = q.shape
    dv = v.shape[-1]
    c = GDN_CHUNK
    n_chunks = seq // c
    f32 = jnp.float32

    def to_chunks(t):
        t = jnp.moveaxis(t.astype(f32), 2, 1)
        return t.reshape((bsz, heads, n_chunks, c) + t.shape[3:])

    q = to_chunks(q) * (dk ** -0.5)
    k = to_chunks(k)
    v = to_chunks(v)
    g = to_chunks(g)
    beta = to_chunks(beta)

    g_cum = jnp.cumsum(g, axis=-1)
    incl = jnp.tril(jnp.ones((c, c), bool))
    strict = jnp.tril(jnp.ones((c, c), bool), -1)
    diff = g_cum[..., :, None] - g_cum[..., None, :]
    decay = jnp.where(incl, jnp.exp(jnp.where(incl, diff, 0.0)), 0.0)

    k_beta = k * beta[..., None]
    v_beta = v * beta[..., None]
    lower = jnp.where(strict, jnp.einsum("bhnid,bhnjd->bhnij", k_beta, k) * decay, 0.0)
    eye = jnp.eye(c, dtype=f32)
    t_inv = lax_linalg.triangular_solve(eye + lower, jnp.broadcast_to(eye, lower.shape),
                                        left_side=True, lower=True, unit_diagonal=True)
    u = t_inv @ v_beta
    w = t_inv @ (k_beta * jnp.exp(g_cum)[..., None])
    a_intra = jnp.einsum("bhnid,bhnjd->bhnij", q, k) * decay
    q_g = q * jnp.exp(g_cum)[..., None]
    g_last = g_cum[..., -1]
    k_g = k * jnp.exp(g_last[..., None] - g_cum)[..., None]

    xs = tuple(jnp.moveaxis(t, 2, 0) for t in (u, w, a_intra, q_g, k_g, g_last))

    def step(state, inp):
        u_c, w_c, a_c, qg_c, kg_c, gl_c = inp
        v_new = u_c - jnp.einsum("bhck,bhkv->bhcv", w_c, state)
        o_c = jnp.einsum("bhck,bhkv->bhcv", qg_c, state) + jnp.einsum("bhij,bhjv->bhiv", a_c, v_new)
        state = state * jnp.exp(gl_c)[..., None, None] + jnp.einsum("bhck,bhcv->bhkv", kg_c, v_new)
        return state, o_c

    s0 = jnp.zeros((bsz, heads, dk, dv), f32)
    _, o = lax.scan(step, s0, xs)
    return jnp.transpose(o, (1, 0, 3, 2, 4)).reshape(bsz, seq, heads, dv)


def moba_attention(q, k, v, rel_bias):
    bsz, seq, heads, d = q.shape
    f32 = jnp.float32
    n_blk = -(-seq // MOBA_BLOCK)
    t_pad = n_blk * MOBA_BLOCK

    def prep(t):
        t = jnp.moveaxis(t.astype(f32), 1, 2)
        return jnp.pad(t, ((0, 0), (0, 0), (0, t_pad - seq), (0, 0)))

    q = prep(q) * (d ** -0.5)
    k = prep(k)
    v = prep(v)
    kb = k.reshape(bsz, heads, n_blk, MOBA_BLOCK, d)
    vb = v.reshape(bsz, heads, n_blk, MOBA_BLOCK, d)
    k_mean = jnp.mean(kb, axis=3)

    qblk = jnp.arange(t_pad) // MOBA_BLOCK
    gate = jnp.einsum("bhtd,bhnd->bhtn", q, k_mean)
    fully_past = jnp.arange(n_blk)[None, :] < qblk[:, None]
    gate = jnp.where(fully_past, gate, NEG_INF)
    n_sel = min(MOBA_TOPK, n_blk)
    _, sel = lax.top_k(gate, n_sel)

    n_qc = t_pad // MOBA_QCHUNK
    q_c = q.reshape(bsz, heads, n_qc, MOBA_QCHUNK, d).transpose(2, 0, 1, 3, 4)
    sel_c = sel.reshape(bsz, heads, n_qc, MOBA_QCHUNK, n_sel).transpose(2, 0, 1, 3, 4)
    bias_hb = rel_bias.astype(f32).T
    b_idx = jnp.arange(bsz)[:, None, None, None]
    h_idx = jnp.arange(heads)[None, :, None, None]
    offs = jnp.arange(MOBA_BLOCK)

    def q_chunk(args):
        qc, selc, ci = args
        t0 = ci * MOBA_QCHUNK
        tpos = t0 + jnp.arange(MOBA_QCHUNK)
        start = (t0 // MOBA_BLOCK) * MOBA_BLOCK
        k_own = lax.dynamic_slice_in_dim(k, start, MOBA_BLOCK, axis=2)
        v_own = lax.dynamic_slice_in_dim(v, start, MOBA_BLOCK, axis=2)
        d_own = tpos[:, None] - (start + offs)[None, :]
        s_own = jnp.einsum("bhqd,bhkd->bhqk", qc, k_own) + bias_hb[:, rel_bucket(d_own)]
        s_own = jnp.where(d_own >= 0, s_own, NEG_INF)
        k_sel = kb[b_idx, h_idx, selc]
        v_sel = vb[b_idx, h_idx, selc]
        d_sel = tpos[:, None, None] - (selc[..., None] * MOBA_BLOCK + offs)
        s_sel = (jnp.einsum("bhqd,bhqnkd->bhqnk", qc, k_sel)
                 + bias_hb[h_idx[..., None], rel_bucket(d_sel)])
        valid = jnp.arange(n_sel)[None, :] < (tpos // MOBA_BLOCK)[:, None]
        s_sel = jnp.where(valid[:, :, None], s_sel, NEG_INF)
        logits = jnp.concatenate(
            [s_own, s_sel.reshape(bsz, heads, MOBA_QCHUNK, n_sel * MOBA_BLOCK)], axis=-1)
        p = jax.nn.softmax(logits, axis=-1)
        p_own = p[..., :MOBA_BLOCK]
        p_sel = p[..., MOBA_BLOCK:].reshape(s_sel.shape)
        return (jnp.einsum("bhqk,bhkd->bhqd", p_own, v_own)
                + jnp.einsum("bhqnk,bhqnkd->bhqd", p_sel, v_sel))

    o = lax.map(q_chunk, (q_c, sel_c, jnp.arange(n_qc)))
    return o.transpose(1, 0, 3, 2, 4).reshape(bsz, t_pad, heads, d)[:, :seq]


def hybrid_mixer(h, w_in, conv_qkv_w, a_log, dt_bias, gdn_norm_w, conv_c_w, w_out, rel_bias):
    bsz, seq, _ = h.shape
    dt = h.dtype
    proj = h @ w_in
    (q_a, k_a, v_a, z_a, b_a, a_a, q_b, k_b, v_b, x_c, bg_c, cg_c) = split_cols(proj, PROJ_SIZES)

    qkv = jax.nn.silu(causal_depthwise_conv(jnp.concatenate([q_a, k_a, v_a], axis=-1), conv_qkv_w))
    q_a, k_a, v_a = split_cols(qkv, (GDN_WIDTH,) * 3)
    hs = (bsz, seq, GDN_HEADS, GDN_HEAD_DIM)
    q_a = l2norm(q_a.reshape(hs))
    k_a = l2norm(k_a.reshape(hs))
    v_a = v_a.reshape(hs)
    beta = jax.nn.sigmoid(b_a.astype(jnp.float32))
    g = -jnp.exp(a_log.astype(jnp.float32)) * jax.nn.softplus(
        a_a.astype(jnp.float32) + dt_bias.astype(jnp.float32))
    o_a = gated_delta_rule(q_a, k_a, v_a, g, beta)
    o_a = rmsnorm(o_a, gdn_norm_w) * jax.nn.silu(z_a.reshape(hs).astype(jnp.float32))
    o_a = o_a.astype(dt).reshape(bsz, seq, GDN_WIDTH)

    bs = (bsz, seq, MOBA_HEADS, MOBA_HEAD_DIM)
    o_b = moba_attention(q_b.reshape(bs), k_b.reshape(bs), v_b.reshape(bs), rel_bias)
    o_b = o_b.astype(dt).reshape(bsz, seq, MOBA_WIDTH)

    o_c = bg_c * causal_depthwise_conv(cg_c * x_c, conv_c_w)

    return jnp.concatenate([o_a, o_b, o_c.astype(dt)], axis=-1) @ w_out


def setup_inputs(seed: int = 0) -> dict:
    key = jax.random.key(seed)
    ks = jax.random.split(key, 20)
    L, D, F = DEPTH, D_MODEL, D_FF

    def nrm(k, shape, scale):
        return jax.random.normal(k, shape, jnp.float32) * scale

    def gain(k, shape):
        return 1.0 + 0.02 * jax.random.normal(k, shape, jnp.float32)

    dt = jnp.exp(jax.random.uniform(ks[9], (L, GDN_HEADS), jnp.float32,
                                    minval=math.log(1e-3), maxval=math.log(1e-1)))
    return {
        "x": nrm(ks[0], (BATCH, SEQ, D), 1.0),
        "ffn1_norm": gain(ks[1], (L, D)),
        "ffn1_w_gate": nrm(ks[2], (L, D, F), D ** -0.5),
        "ffn1_w_up": nrm(ks[3], (L, D, F), D ** -0.5),
        "ffn1_w_down": nrm(ks[4], (L, F, D), F ** -0.5),
        "mix_norm": gain(ks[5], (L, D)),
        "w_in": nrm(ks[6], (L, D, PROJ_WIDTH), D ** -0.5),
        "conv_qkv_w": nrm(ks[7], (L, GDN_CONV, 3 * GDN_WIDTH), GDN_CONV ** -0.5),
        "a_log": jnp.log(jax.random.uniform(ks[8], (L, GDN_HEADS), jnp.float32, minval=1.0, maxval=16.0)),
        "dt_bias": dt + jnp.log(-jnp.expm1(-dt)),
        "gdn_norm_w": gain(ks[10], (L, GDN_HEAD_DIM)),
        "conv_c_w": nrm(ks[11], (L, SCONV_K, SCONV_WIDTH), SCONV_K ** -0.5),
        "w_out": nrm(ks[12], (L, D, D), D ** -0.5),
        "ffn2_norm": gain(ks[13], (L, D)),
        "ffn2_w_gate": nrm(ks[14], (L, D, F), D ** -0.5),
        "ffn2_w_up": nrm(ks[15], (L, D, F), D ** -0.5),
        "ffn2_w_down": nrm(ks[16], (L, F, D), F ** -0.5),
        "rel_bias": nrm(ks[17], (REL_BUCKETS, MOBA_HEADS), 0.2),
        "final_norm": gain(ks[18], (D,)),
    }


def reference(x, ffn1_norm, ffn1_w_gate, ffn1_w_up, ffn1_w_down, mix_norm, w_in, conv_qkv_w,
              a_log, dt_bias, gdn_norm_w, conv_c_w, w_out, ffn2_norm, ffn2_w_gate, ffn2_w_up,
              ffn2_w_down, rel_bias, final_norm):
    for l in range(DEPTH):
        x = x + 0.5 * swiglu(rmsnorm(x, ffn1_norm[l]), ffn1_w_gate[l], ffn1_w_up[l], ffn1_w_down[l])
        x = x + hybrid_mixer(rmsnorm(x, mix_norm[l]), w_in[l], conv_qkv_w[l], a_log[l], dt_bias[l],
                             gdn_norm_w[l], conv_c_w[l], w_out[l], rel_bias)
        x = x + 0.5 * swiglu(rmsnorm(x, ffn2_norm[l]), ffn2_w_gate[l], ffn2_w_up[l], ffn2_w_down[l])
    return rmsnorm(x, final_norm)
```

```python
import functools
import math

import jax
import jax.numpy as jnp
import numpy as np
from jax import lax
from jax.experimental import pallas as pl
from jax.experimental.pallas import tpu as pltpu

F32 = jnp.float32
BF16 = jnp.bfloat16

D_MODEL = 2048
DEPTH = 2
EPS = 1e-6
D_FF = 5632

GDN_HEAD_DIM = 128
GDN_HEADS = 8
GDN_WIDTH = GDN_HEADS * GDN_HEAD_DIM
GDN_CONV = 4
GDN_CHUNK = 64

MOBA_HEAD_DIM = 128
MOBA_HEADS = 4
MOBA_WIDTH = MOBA_HEADS * MOBA_HEAD_DIM
MOBA_BLOCK = 256
MOBA_TOPK = 3

SCONV_WIDTH = D_MODEL - GDN_WIDTH - MOBA_WIDTH
SCONV_K = 3

REL_BUCKETS = 32
REL_MAX_DIST = 128
NEG_INF = -1e30

LANES = 128
CONV_HALO = 8

PROJ_MAIN = 4 * GDN_WIDTH + 3 * MOBA_WIDTH + 3 * SCONV_WIDTH
COL_B = 4 * GDN_WIDTH
COL_C = COL_B + 3 * MOBA_WIDTH

VMEM_LIMIT = 56 * 1024 * 1024

TM_FFN = 512
TF_FFN = 512
TM_PROJ = 512
TN_PROJ = 1024
TM_OUT = 512
TB_GDN = 256


def _cparams(sem):
    return pltpu.CompilerParams(dimension_semantics=sem, vmem_limit_bytes=VMEM_LIMIT)


def _rms(x, w):
    return x * lax.rsqrt(jnp.mean(x * x, axis=-1, keepdims=True) + EPS) * w


def _split3(x):
    x1 = x.astype(BF16)
    r1 = x - x1.astype(F32)
    x2 = r1.astype(BF16)
    x3 = (r1 - x2.astype(F32)).astype(BF16)
    return x1, x2, x3


def _dot(a, b):
    return jnp.dot(a, b, preferred_element_type=F32)


def _dot_nt(a, b):
    return lax.dot_general(a, b, (((1,), (1,)), ((), ())), preferred_element_type=F32)


def _dot_tn(a, b):
    return lax.dot_general(a, b, (((0,), (0,)), ((), ())), preferred_element_type=F32)


def _ffn_kernel(x_ref, nw_ref, wg_ref, wu_ref, wd_ref, o_ref, hn_ref, acc_ref, *, n_f):
    f = pl.program_id(1)

    @pl.when(f == 0)
    def _():
        hn_ref[...] = _rms(x_ref[...], nw_ref[...]).astype(BF16)
        acc_ref[...] = jnp.zeros_like(acc_ref)

    hn = hn_ref[...]
    g = _dot(hn, wg_ref[...])
    u = _dot(hn, wu_ref[...])
    a = (g * jax.nn.sigmoid(g) * u).astype(BF16)
    acc_ref[...] += _dot(a, wd_ref[...])

    @pl.when(f == n_f - 1)
    def _():
        o_ref[...] = x_ref[...] + 0.5 * acc_ref[...]


def _ffn(x, nw, wg, wu, wd):
    n, d = x.shape
    ff = wg.shape[1]
    n_f = ff // TF_FFN
    return pl.pallas_call(
        functools.partial(_ffn_kernel, n_f=n_f),
        out_shape=jax.ShapeDtypeStruct((n, d), F32),
        grid=(n // TM_FFN, n_f),
        in_specs=[
            pl.BlockSpec((TM_FFN, d), lambda i, f: (i, 0)),
            pl.BlockSpec((1, d), lambda i, f: (0, 0)),
            pl.BlockSpec((d, TF_FFN), lambda i, f: (0, f)),
            pl.BlockSpec((d, TF_FFN), lambda i, f: (0, f)),
            pl.BlockSpec((TF_FFN, d), lambda i, f: (f, 0)),
        ],
        out_specs=pl.BlockSpec((TM_FFN, d), lambda i, f: (i, 0)),
        scratch_shapes=[pltpu.VMEM((TM_FFN, d), BF16), pltpu.VMEM((TM_FFN, d), F32)],
        compiler_params=_cparams(("parallel", "arbitrary")),
        name="ffn",
    )(x, nw.reshape(1, d), wg, wu, wd)


def _proj_kernel(x_ref, nw_ref, w_ref, wba_ref, o_ref, ba_ref, hn_ref):
    n = pl.program_id(1)

    @pl.when(n == 0)
    def _():
        h = _rms(x_ref[...], nw_ref[...])
        h1, h2, _ = _split3(h)
        hn_ref[...] = h1
        w1 = wba_ref[0]
        w2 = wba_ref[1]
        ba_ref[...] = _dot(h1, w1) + (_dot(h1, w2) + _dot(h2, w1))

    o_ref[...] = _dot(hn_ref[...], w_ref[...])


def _proj(x, nw, w_main, w_ba):
    n, d = x.shape
    nm = w_main.shape[1]
    return pl.pallas_call(
        _proj_kernel,
        out_shape=(jax.ShapeDtypeStruct((n, nm), F32), jax.ShapeDtypeStruct((n, LANES), F32)),
        grid=(n // TM_PROJ, nm // TN_PROJ),
        in_specs=[
            pl.BlockSpec((TM_PROJ, d), lambda i, j: (i, 0)),
            pl.BlockSpec((1, d), lambda i, j: (0, 0)),
            pl.BlockSpec((d, TN_PROJ), lambda i, j: (0, j)),
            pl.BlockSpec((2, d, LANES), lambda i, j: (0, 0, 0)),
        ],
        out_specs=(pl.BlockSpec((TM_PROJ, TN_PROJ), lambda i, j: (i, j)),
                   pl.BlockSpec((TM_PROJ, LANES), lambda i, j: (i, 0))),
        scratch_shapes=[pltpu.VMEM((TM_PROJ, d), BF16)],
        compiler_params=_cparams(("parallel", "arbitrary")),
        name="proj",
    )(x, nw.reshape(1, d), w_main, w_ba)


def _conv_silu(x_ref, hist_ref, ext_ref, w_ref, first):
    tb = x_ref.shape[0]

    @pl.when(first)
    def _():
        hist_ref[...] = jnp.zeros_like(hist_ref)

    x = x_ref[...]
    ext_ref[0:CONV_HALO, :] = hist_ref[...]
    ext_ref[CONV_HALO:, :] = x
    hist_ref[...] = x[tb - CONV_HALO:, :]
    width = w_ref.shape[0]
    y = x * w_ref[width - 1:width, :]
    for s in range(1, width):
        y = y + ext_ref[CONV_HALO - s:CONV_HALO - s + tb, :] * w_ref[width - 1 - s:width - s, :]
    return y * jax.nn.sigmoid(y)


def _gdn_kernel(q_ref, k_ref, v_ref, z_ref, ba_ref, wq_ref, wk_ref, wv_ref, alog_ref, dtb_ref, nw_ref,
                o_ref,
                hq_ref, hk_ref, hv_ref, ext_ref, qc_ref, kc_ref, vc_ref, gc_ref, gct_ref, beta_ref,
                s_ref, vn_ref, oh_ref):
    t = pl.program_id(1)
    tb = q_ref.shape[0]
    nchunk = tb // GDN_CHUNK
    dk = GDN_HEAD_DIM
    first = t == 0

    qc_ref[...] = _conv_silu(q_ref, hq_ref, ext_ref, wq_ref, first)
    kc_ref[...] = _conv_silu(k_ref, hk_ref, ext_ref, wk_ref, first)
    vc_ref[...] = _conv_silu(v_ref, hv_ref, ext_ref, wv_ref, first)

    @pl.when(first)
    def _():
        s_ref[...] = jnp.zeros_like(s_ref)

    ba = ba_ref[...]
    beta_ref[...] = jax.nn.sigmoid(ba)
    xg = ba + dtb_ref[...]
    softplus = jnp.maximum(xg, 0.0) + jnp.log1p(jnp.exp(-jnp.abs(xg)))
    g = -jnp.exp(alog_ref[...]) * softplus
    row = lax.broadcasted_iota(jnp.int32, (tb, tb), 0)
    col = lax.broadcasted_iota(jnp.int32, (tb, tb), 1)
    shift = int(math.log2(GDN_CHUNK))
    same = jnp.right_shift(row, shift) == jnp.right_shift(col, shift)
    incl = same & (col <= row)
    strict = same & (col < row)
    tri = jnp.where(incl, 1.0, 0.0).astype(BF16)
    g1, g2, g3 = _split3(g)
    gc = _dot(tri, g1) + (_dot(tri, g2) + _dot(tri, g3))
    gc_ref[...] = gc
    gct_ref[...] = gc.T

    lane = lax.broadcasted_iota(jnp.int32, (tb, LANES), 1)
    eye = jnp.where(row == col, 1.0, 0.0)

    def head(h, carry):
        c0 = pl.multiple_of(h * dk, dk)
        q = qc_ref[:, pl.ds(c0, dk)]
        k = kc_ref[:, pl.ds(c0, dk)]
        v = vc_ref[:, pl.ds(c0, dk)]
        q = q * lax.rsqrt(jnp.sum(q * q, axis=-1, keepdims=True) + EPS) * (dk ** -0.5)
        k = k * lax.rsqrt(jnp.sum(k * k, axis=-1, keepdims=True) + EPS)
        gcol = jnp.sum(jnp.where(lane == GDN_HEADS + h, gc_ref[...], 0.0), axis=-1, keepdims=True)
        bcol = jnp.sum(jnp.where(lane == h, beta_ref[...], 0.0), axis=-1, keepdims=True)
        grow = gct_ref[pl.ds(GDN_HEADS + h, 1), :]

        decay = jnp.where(incl, jnp.exp(jnp.where(incl, gcol - grow, 0.0)), 0.0)
        kb = k * bcol
        k16 = k.astype(BF16)
        kk = _dot_nt(kb.astype(BF16), k16)
        qk = _dot_nt(q.astype(BF16), k16)
        a_intra = (qk * decay).astype(BF16)
        m = -jnp.where(strict, kk * decay, 0.0)
        p = eye + m
        for _ in range(int(math.log2(GDN_CHUNK)) - 1):
            m16 = m.astype(BF16)
            m = _dot(m16, m16)
            p = p + _dot(p.astype(BF16), m.astype(BF16))
        eg = jnp.exp(gcol)
        rhs = jnp.concatenate([v * bcol, kb * eg], axis=1).astype(BF16)
        uw = _dot(p.astype(BF16), rhs)
        u = uw[:, :dk]
        w16 = uw[:, dk:].astype(BF16)
        qg16 = (q * eg).astype(BF16)

        vn_ref[...] = jnp.zeros_like(vn_ref)
        s = s_ref[h]
        for c in range(nchunk):
            r0, r1 = c * GDN_CHUNK, (c + 1) * GDN_CHUNK
            s16 = s.astype(BF16)
            v_new = u[r0:r1] - _dot(w16[r0:r1], s16)
            vn_ref[r0:r1, :] = v_new.astype(BF16)
            oh_ref[r0:r1, :] = _dot(qg16[r0:r1], s16) + _dot(a_intra[r0:r1], vn_ref[...])
            gl = gcol[r1 - 1:r1, :]
            kg = k[r0:r1] * jnp.exp(gl - gcol[r0:r1])
            s = s * jnp.exp(gl) + _dot_tn(kg.astype(BF16), v_new.astype(BF16))
        s_ref[h] = s

        o = oh_ref[...]
        z = z_ref[:, pl.ds(c0, dk)]
        o = _rms(o, nw_ref[...]) * (z * jax.nn.sigmoid(z))
        o_ref[:, pl.ds(c0, dk)] = o.astype(o_ref.dtype)
        return carry

    lax.fori_loop(0, GDN_HEADS, head, 0)


def _gdn(proj, ba, conv_w, a_log, dt_bias, norm_w, bsz, seq):
    tb = TB_GDN
    nt = seq // tb
    gw = GDN_WIDTH
    alog = jnp.zeros((1, LANES), F32).at[0, GDN_HEADS:2 * GDN_HEADS].set(a_log)
    dtb = jnp.zeros((1, LANES), F32).at[0, GDN_HEADS:2 * GDN_HEADS].set(dt_bias)

    def rows(b, t):
        return b * nt + t

    return pl.pallas_call(
        _gdn_kernel,
        out_shape=jax.ShapeDtypeStruct((bsz * seq, gw), BF16),
        grid=(bsz, nt),
        in_specs=[
            pl.BlockSpec((tb, gw), lambda b, t: (rows(b, t), 0)),
            pl.BlockSpec((tb, gw), lambda b, t: (rows(b, t), 1)),
            pl.BlockSpec((tb, gw), lambda b, t: (rows(b, t), 2)),
            pl.BlockSpec((tb, gw), lambda b, t: (rows(b, t), 3)),
            pl.BlockSpec((tb, LANES), lambda b, t: (rows(b, t), 0)),
            pl.BlockSpec((GDN_CONV, gw), lambda b, t: (0, 0)),
            pl.BlockSpec((GDN_CONV, gw), lambda b, t: (0, 1)),
            pl.BlockSpec((GDN_CONV, gw), lambda b, t: (0, 2)),
            pl.BlockSpec((1, LANES), lambda b, t: (0, 0)),
            pl.BlockSpec((1, LANES), lambda b, t: (0, 0)),
            pl.BlockSpec((1, GDN_HEAD_DIM), lambda b, t: (0, 0)),
        ],
        out_specs=pl.BlockSpec((tb, gw), lambda b, t: (rows(b, t), 0)),
        scratch_shapes=[
            pltpu.VMEM((CONV_HALO, gw), F32), pltpu.VMEM((CONV_HALO, gw), F32), pltpu.VMEM((CONV_HALO, gw), F32),
            pltpu.VMEM((CONV_HALO + tb, gw), F32),
            pltpu.VMEM((tb, gw), F32), pltpu.VMEM((tb, gw), F32), pltpu.VMEM((tb, gw), F32),
            pltpu.VMEM((tb, LANES), F32), pltpu.VMEM((LANES, tb), F32), pltpu.VMEM((tb, LANES), F32),
            pltpu.VMEM((GDN_HEADS, GDN_HEAD_DIM, GDN_HEAD_DIM), F32),
            pltpu.VMEM((tb, GDN_HEAD_DIM), BF16),
            pltpu.VMEM((tb, GDN_HEAD_DIM), F32),
        ],
        compiler_params=_cparams(("parallel", "arbitrary")),
        name="gdn",
    )(proj, proj, proj, proj, ba, conv_w, conv_w, conv_w, alog, dtb, norm_w.reshape(1, GDN_HEAD_DIM))


def _kmean_kernel(k_ref, o_ref):
    nb = o_ref.shape[0]
    k = k_ref[...]
    o_ref[...] = jnp.mean(k.reshape(nb, MOBA_BLOCK, k.shape[-1]), axis=1)


def _kmean(proj, bsz, seq):
    nb = seq // MOBA_BLOCK
    return pl.pallas_call(
        _kmean_kernel,
        out_shape=jax.ShapeDtypeStruct((bsz * nb, MOBA_WIDTH), F32),
        grid=(bsz,),
        in_specs=[pl.BlockSpec((seq, MOBA_WIDTH), lambda b: (b, COL_B // MOBA_WIDTH + 1))],
        out_specs=pl.BlockSpec((nb, MOBA_WIDTH), lambda b: (b, 0)),
        compiler_params=_cparams(("parallel",)),
        name="kmean",
    )(proj)


def _moba_kernel(qi_ref, j_ref, q_ref, k_ref, v_ref, km_ref, bown_ref, bprev_ref, bfar_ref, o_ref,
                 m_ref, l_ref, acc_ref, msk_ref):
    step = pl.program_id(1)
    qi = qi_ref[step]
    j = j_ref[step]
    blk = MOBA_BLOCK
    hd = MOBA_HEAD_DIM
    nb = km_ref.shape[0]
    scale = hd ** -0.5

    for h in range(MOBA_HEADS):
        cs = slice(h * hd, (h + 1) * hd)
        q = q_ref[:, cs] * scale

        @pl.when(j == 0)
        def _():
            m_ref[h] = jnp.full((blk, 1), NEG_INF, F32)
            l_ref[h] = jnp.zeros((blk, 1), F32)
            acc_ref[h] = jnp.zeros((blk, hd), F32)
            km = jnp.concatenate([km_ref[:, cs], jnp.zeros((LANES - nb, hd), F32)], axis=0)
            q1, q2, _ = _split3(q)
            k1, k2, _ = _split3(km)
            gate = _dot_nt(q1, k1) + (_dot_nt(q1, k2) + _dot_nt(q2, k1))
            lane = lax.broadcasted_iota(jnp.int32, (blk, LANES), 1)
            past = lane < qi
            for i in range(nb):
                gi = gate[:, i:i + 1]
                ahead = past & ((gate > gi) | ((gate == gi) & (lane < i)))
                cnt = jnp.sum(jnp.where(ahead, 1.0, 0.0), axis=-1, keepdims=True)
                msk_ref[h, i] = jnp.broadcast_to(jnp.where(cnt < MOBA_TOPK, 0.0, NEG_INF), (blk, LANES))

        s = _dot_nt(q.astype(BF16), k_ref[:, cs].astype(BF16))
        v16 = v_ref[:, cs].astype(BF16)

        def update(s):
            m_prev = m_ref[h]
            m_new = jnp.maximum(m_prev, jnp.max(s, axis=-1, keepdims=True))
            alpha = jnp.exp(m_prev - m_new)
            p = jnp.exp(s - m_new)
            l_ref[h] = alpha * l_ref[h] + jnp.sum(p, axis=-1, keepdims=True)
            acc_ref[h] = alpha * acc_ref[h] + _dot(p.astype(BF16), v16)
            m_ref[h] = m_new

        @pl.when(j == 0)
        def _():
            update(s + bown_ref[h])

        def past_bias():
            mb = msk_ref[h, qi - j]
            return jnp.concatenate([mb] * (blk // LANES), axis=1)

        @pl.when(j == 1)
        def _():
            update(s + bprev_ref[h] + past_bias())

        @pl.when(j >= 2)
        def _():
            update(s + bfar_ref[h] + past_bias())

        @pl.when(j == qi)
        def _():
            o_ref[:, cs] = (acc_ref[h] / l_ref[h]).astype(o_ref.dtype)


def _rel_bucket(dist):
    n = jnp.maximum(dist, 0)
    max_exact = REL_BUCKETS // 2
    nf = jnp.maximum(n, max_exact).astype(F32)
    large = max_exact + (jnp.log(nf / max_exact) / math.log(REL_MAX_DIST / max_exact)
                         * (REL_BUCKETS - max_exact)).astype(jnp.int32)
    large = jnp.minimum(large, REL_BUCKETS - 1)
    return jnp.where(n < max_exact, n, large)


def _moba(proj, kmean, rel_bias, bsz, seq):
    blk = MOBA_BLOCK
    nb = seq // blk
    assert blk >= REL_MAX_DIST
    r = jnp.arange(blk)[:, None]
    c = jnp.arange(blk)[None, :]
    bias_hb = rel_bias.astype(F32).T
    b_own = jnp.where((r >= c)[None], bias_hb[:, _rel_bucket(r - c)], NEG_INF)
    b_prev = bias_hb[:, _rel_bucket(blk + r - c)]
    b_far = jnp.broadcast_to(bias_hb[:, _rel_bucket(jnp.full((1, 1), 2 * blk))], (MOBA_HEADS, 1, blk))
    qi_list, j_list = [], []
    for qi in range(nb):
        for j in range(qi + 1):
            qi_list.append(qi)
            j_list.append(j)
    qi_arr = jnp.asarray(np.array(qi_list, np.int32))
    j_arr = jnp.asarray(np.array(j_list, np.int32))
    cb = COL_B // MOBA_WIDTH

    grid_spec = pltpu.PrefetchScalarGridSpec(
        num_scalar_prefetch=2,
        grid=(bsz, len(qi_list)),
        in_specs=[
            pl.BlockSpec((blk, MOBA_WIDTH), lambda b, s, qi, j: (b * nb + qi[s], cb)),
            pl.BlockSpec((blk, MOBA_WIDTH), lambda b, s, qi, j: (b * nb + qi[s] - j[s], cb + 1)),
            pl.BlockSpec((blk, MOBA_WIDTH), lambda b, s, qi, j: (b * nb + qi[s] - j[s], cb + 2)),
            pl.BlockSpec((nb, MOBA_WIDTH), lambda b, s, qi, j: (b, 0)),
            pl.BlockSpec((MOBA_HEADS, blk, blk), lambda b, s, qi, j: (0, 0, 0)),
            pl.BlockSpec((MOBA_HEADS, blk, blk), lambda b, s, qi, j: (0, 0, 0)),
            pl.BlockSpec((MOBA_HEADS, 1, blk), lambda b, s, qi, j: (0, 0, 0)),
        ],
        out_specs=pl.BlockSpec((blk, MOBA_WIDTH), lambda b, s, qi, j: (b * nb + qi[s], 0)),
        scratch_shapes=[
            pltpu.VMEM((MOBA_HEADS, blk, 1), F32),
            pltpu.VMEM((MOBA_HEADS, blk, 1), F32),
            pltpu.VMEM((MOBA_HEADS, blk, MOBA_HEAD_DIM), F32),
            pltpu.VMEM((MOBA_HEADS, nb, blk, LANES), F32),
        ],
    )
    return pl.pallas_call(
        _moba_kernel,
        out_shape=jax.ShapeDtypeStruct((bsz * seq, MOBA_WIDTH), BF16),
        grid_spec=grid_spec,
        compiler_params=_cparams(("parallel", "arbitrary")),
        name="moba",
    )(qi_arr, j_arr, proj, proj, proj, kmean, b_own, b_prev, b_far)


def _sconv_kernel(x_ref, bg_ref, cg_ref, w_ref, o_ref, ext_ref):
    seq = x_ref.shape[0]
    u = cg_ref[...] * x_ref[...]
    ext_ref[0:CONV_HALO, :] = jnp.zeros((CONV_HALO, u.shape[1]), F32)
    ext_ref[CONV_HALO:, :] = u
    width = w_ref.shape[0]
    y = u * w_ref[width - 1:width, :]
    for s in range(1, width):
        y = y + ext_ref[CONV_HALO - s:CONV_HALO - s + seq, :] * w_ref[width - 1 - s:width - s, :]
    o_ref[...] = (bg_ref[...] * y).astype(o_ref.dtype)


def _sconv(proj, conv_w, bsz, seq):
    cw = SCONV_WIDTH
    nc = cw // LANES
    c0 = COL_C // LANES
    return pl.pallas_call(
        _sconv_kernel,
        out_shape=jax.ShapeDtypeStruct((bsz * seq, cw), BF16),
        grid=(bsz, nc),
        in_specs=[
            pl.BlockSpec((seq, LANES), lambda b, c: (b, c0 + c)),
            pl.BlockSpec((seq, LANES), lambda b, c: (b, c0 + nc + c)),
            pl.BlockSpec((seq, LANES), lambda b, c: (b, c0 + 2 * nc + c)),
            pl.BlockSpec((SCONV_K, LANES), lambda b, c: (0, c)),
        ],
        out_specs=pl.BlockSpec((seq, LANES), lambda b, c: (b, c)),
        scratch_shapes=[pltpu.VMEM((CONV_HALO + seq, LANES), F32)],
        compiler_params=_cparams(("parallel", "parallel")),
        name="sconv",
    )(proj, proj, proj, conv_w)


def _out_kernel(x_ref, oa_ref, ob_ref, oc_ref, w_ref, o_ref):
    a0, a1, a2 = GDN_WIDTH, GDN_WIDTH + MOBA_WIDTH, D_MODEL
    y = _dot(oa_ref[...], w_ref[0:a0, :])
    y = y + _dot(ob_ref[...], w_ref[a0:a1, :])
    y = y + _dot(oc_ref[...], w_ref[a1:a2, :])
    o_ref[...] = x_ref[...] + y


def _out(x, oa, ob, oc, w):
    n, d = x.shape
    return pl.pallas_call(
        _out_kernel,
        out_shape=jax.ShapeDtypeStruct((n, d), F32),
        grid=(n // TM_OUT,),
        in_specs=[
            pl.BlockSpec((TM_OUT, d), lambda i: (i, 0)),
            pl.BlockSpec((TM_OUT, GDN_WIDTH), lambda i: (i, 0)),
            pl.BlockSpec((TM_OUT, MOBA_WIDTH), lambda i: (i, 0)),
            pl.BlockSpec((TM_OUT, SCONV_WIDTH), lambda i: (i, 0)),
            pl.BlockSpec((d, d), lambda i: (0, 0)),
        ],
        out_specs=pl.BlockSpec((TM_OUT, d), lambda i: (i, 0)),
        compiler_params=_cparams(("parallel",)),
        name="out_proj",
    )(x, oa, ob, oc, w)


def _norm_kernel(x_ref, w_ref, o_ref):
    o_ref[...] = _rms(x_ref[...], w_ref[...])


def _final_norm(x, w):
    n, d = x.shape
    return pl.pallas_call(
        _norm_kernel,
        out_shape=jax.ShapeDtypeStruct((n, d), F32),
        grid=(n // TM_OUT,),
        in_specs=[pl.BlockSpec((TM_OUT, d), lambda i: (i, 0)), pl.BlockSpec((1, d), lambda i: (0, 0))],
        out_specs=pl.BlockSpec((TM_OUT, d), lambda i: (i, 0)),
        compiler_params=_cparams(("parallel",)),
        name="final_norm",
    )(x, w.reshape(1, d))


def _mixer(x, mix_norm, w_in, conv_qkv_w, a_log, dt_bias, gdn_norm_w, conv_c_w, w_out, rel_bias, bsz, seq):
    a_end = 4 * GDN_WIDTH
    ba_end = a_end + 2 * GDN_HEADS
    w_main = jnp.concatenate([w_in[:, :a_end], w_in[:, ba_end:]], axis=1).astype(BF16)
    w_ba = jnp.pad(w_in[:, a_end:ba_end], ((0, 0), (0, LANES - 2 * GDN_HEADS)))
    w_ba1 = w_ba.astype(BF16)
    w_ba2 = (w_ba - w_ba1.astype(F32)).astype(BF16)
    proj, ba = _proj(x, mix_norm, w_main, jnp.stack([w_ba1, w_ba2]))
    o_a = _gdn(proj, ba, conv_qkv_w, a_log, dt_bias, gdn_norm_w, bsz, seq)
    o_b = _moba(proj, _kmean(proj, bsz, seq), rel_bias, bsz, seq)
    o_c = _sconv(proj, conv_c_w, bsz, seq)
    return _out(x, o_a, o_b, o_c, w_out.astype(BF16))


def kernel(x, ffn1_norm, ffn1_w_gate, ffn1_w_up, ffn1_w_down, mix_norm, w_in, conv_qkv_w, a_log, dt_bias,
           gdn_norm_w, conv_c_w, w_out, ffn2_norm, ffn2_w_gate, ffn2_w_up, ffn2_w_down, rel_bias, final_norm):
    bsz, seq, d = x.shape
    x = x.reshape(bsz * seq, d)
    for l in range(DEPTH):
        x = _ffn(x, ffn1_norm[l], ffn1_w_gate[l].astype(BF16), ffn1_w_up[l].astype(BF16),
                 ffn1_w_down[l].astype(BF16))
        x = _mixer(x, mix_norm[l], w_in[l], conv_qkv_w[l], a_log[l], dt_bias[l], gdn_norm_w[l], conv_c_w[l],
                   w_out[l], rel_bias, bsz, seq)
        x = _ffn(x, ffn2_norm[l], ffn2_w_gate[l].astype(BF16), ffn2_w_up[l].astype(BF16),
                 ffn2_w_down[l].astype(BF16))
    return _final_norm(x, final_norm).reshape(bsz, seq, d)
```

```python
import functools
import math

import jax
import jax.numpy as jnp
import numpy as np
from jax import lax
from jax.experimental import pallas as pl
from jax.experimental.pallas import tpu as pltpu

F32 = jnp.float32
BF16 = jnp.bfloat16

D_MODEL = 2048
DEPTH = 2
EPS = 1e-6
D_FF = 5632

GDN_HEAD_DIM = 128
GDN_HEADS = 8
GDN_WIDTH = GDN_HEADS * GDN_HEAD_DIM
GDN_CONV = 4
GDN_CHUNK = 64

MOBA_HEAD_DIM = 128
MOBA_HEADS = 4
MOBA_WIDTH = MOBA_HEADS * MOBA_HEAD_DIM
MOBA_BLOCK = 256
MOBA_TOPK = 3

SCONV_WIDTH = D_MODEL - GDN_WIDTH - MOBA_WIDTH
SCONV_K = 3

REL_BUCKETS = 32
REL_MAX_DIST = 128
NEG_INF = -1e30

LANES = 128
CONV_HALO = 8

PROJ_MAIN = 4 * GDN_WIDTH + 3 * MOBA_WIDTH + 3 * SCONV_WIDTH
COL_B = 4 * GDN_WIDTH
COL_C = COL_B + 3 * MOBA_WIDTH

VMEM_LIMIT = 56 * 1024 * 1024

TM_FFN = 512
TF_FFN = 512
TM_PROJ = 512
TN_PROJ = 1024
TM_OUT = 512
TB_GDN = 256


def _cparams(sem):
    return pltpu.CompilerParams(dimension_semantics=sem, vmem_limit_bytes=VMEM_LIMIT)


def _rms(x, w):
    return x * lax.rsqrt(jnp.mean(x * x, axis=-1, keepdims=True) + EPS) * w


def _split3(x):
    x1 = x.astype(BF16)
    r1 = x - x1.astype(F32)
    x2 = r1.astype(BF16)
    x3 = (r1 - x2.astype(F32)).astype(BF16)
    return x1, x2, x3


def _dot(a, b):
    return jnp.dot(a, b, preferred_element_type=F32)


def _dot_nt(a, b):
    return lax.dot_general(a, b, (((1,), (1,)), ((), ())), preferred_element_type=F32)


def _dot_tn(a, b):
    return lax.dot_general(a, b, (((0,), (0,)), ((), ())), preferred_element_type=F32)


def _ffn_kernel(x_ref, nw_ref, wg_ref, wu_ref, wd_ref, o_ref, hn_ref, acc_ref, *, n_f):
    f = pl.program_id(1)

    @pl.when(f == 0)
    def _():
        hn_ref[...] = _rms(x_ref[...], nw_ref[...]).astype(BF16)
        acc_ref[...] = jnp.zeros_like(acc_ref)

    hn = hn_ref[...]
    g = _dot(hn, wg_ref[...])
    u = _dot(hn, wu_ref[...])
    a = (g * jax.nn.sigmoid(g) * u).astype(BF16)
    acc_ref[...] += _dot(a, wd_ref[...])

    @pl.when(f == n_f - 1)
    def _():
        o_ref[...] = x_ref[...] + 0.5 * acc_ref[...]


def _ffn(x, nw, wg, wu, wd):
    n, d = x.shape
    ff = wg.shape[1]
    n_f = ff // TF_FFN
    return pl.pallas_call(
        functools.partial(_ffn_kernel, n_f=n_f),
        out_shape=jax.ShapeDtypeStruct((n, d), F32),
        grid=(n // TM_FFN, n_f),
        in_specs=[
            pl.BlockSpec((TM_FFN, d), lambda i, f: (i, 0)),
            pl.BlockSpec((1, d), lambda i, f: (0, 0)),
            pl.BlockSpec((d, TF_FFN), lambda i, f: (0, f)),
            pl.BlockSpec((d, TF_FFN), lambda i, f: (0, f)),
            pl.BlockSpec((TF_FFN, d), lambda i, f: (f, 0)),
        ],
        out_specs=pl.BlockSpec((TM_FFN, d), lambda i, f: (i, 0)),
        scratch_shapes=[pltpu.VMEM((TM_FFN, d), BF16), pltpu.VMEM((TM_FFN, d), F32)],
        compiler_params=_cparams(("parallel", "arbitrary")),
        name="ffn",
    )(x, nw.reshape(1, d), wg, wu, wd)


def _proj_kernel(x_ref, nw_ref, w_ref, wba_ref, o_ref, ba_ref, hn_ref):
    n = pl.program_id(1)

    @pl.when(n == 0)
    def _():
        h = _rms(x_ref[...], nw_ref[...])
        h1, h2, _ = _split3(h)
        hn_ref[...] = h1
        w1 = wba_ref[0]
        w2 = wba_ref[1]
        ba_ref[...] = _dot(h1, w1) + (_dot(h1, w2) + _dot(h2, w1))

    o_ref[...] = _dot(hn_ref[...], w_ref[...])


def _proj(x, nw, w_main, w_ba):
    n, d = x.shape
    nm = w_main.shape[1]
    return pl.pallas_call(
        _proj_kernel,
        out_shape=(jax.ShapeDtypeStruct((n, nm), F32), jax.ShapeDtypeStruct((n, LANES), F32)),
        grid=(n // TM_PROJ, nm // TN_PROJ),
        in_specs=[
            pl.BlockSpec((TM_PROJ, d), lambda i, j: (i, 0)),
            pl.BlockSpec((1, d), lambda i, j: (0, 0)),
            pl.BlockSpec((d, TN_PROJ), lambda i, j: (0, j)),
            pl.BlockSpec((2, d, LANES), lambda i, j: (0, 0, 0)),
        ],
        out_specs=(pl.BlockSpec((TM_PROJ, TN_PROJ), lambda i, j: (i, j)),
                   pl.BlockSpec((TM_PROJ, LANES), lambda i, j: (i, 0))),
        scratch_shapes=[pltpu.VMEM((TM_PROJ, d), BF16)],
        compiler_params=_cparams(("parallel", "arbitrary")),
        name="proj",
    )(x, nw.reshape(1, d), w_main, w_ba)


def _conv_silu(x_ref, hist_ref, ext_ref, w_ref, first):
    tb = x_ref.shape[0]

    @pl.when(first)
    def _():
        hist_ref[...] = jnp.zeros_like(hist_ref)

    x = x_ref[...]
    ext_ref[0:CONV_HALO, :] = hist_ref[...]
    ext_ref[CONV_HALO:, :] = x
    hist_ref[...] = x[tb - CONV_HALO:, :]
    width = w_ref.shape[0]
    y = x * w_ref[width - 1:width, :]
    for s in range(1, width):
        y = y + ext_ref[CONV_HALO - s:CONV_HALO - s + tb, :] * w_ref[width - 1 - s:width - s, :]
    return y * jax.nn.sigmoid(y)


def _gdn_kernel(q_ref, k_ref, v_ref, z_ref, ba_ref, wq_ref, wk_ref, wv_ref, alog_ref, dtb_ref, nw_ref,
                o_ref,
                hq_ref, hk_ref, hv_ref, ext_ref, qc_ref, kc_ref, vc_ref, gc_ref, gct_ref, beta_ref,
                s_ref, vn_ref, oh_ref):
    t = pl.program_id(1)
    tb = q_ref.shape[0]
    nchunk = tb // GDN_CHUNK
    dk = GDN_HEAD_DIM
    first = t == 0

    qc_ref[...] = _conv_silu(q_ref, hq_ref, ext_ref, wq_ref, first)
    kc_ref[...] = _conv_silu(k_ref, hk_ref, ext_ref, wk_ref, first)
    vc_ref[...] = _conv_silu(v_ref, hv_ref, ext_ref, wv_ref, first)

    @pl.when(first)
    def _():
        s_ref[...] = jnp.zeros_like(s_ref)

    ba = ba_ref[...]
    beta_ref[...] = jax.nn.sigmoid(ba)
    xg = ba + dtb_ref[...]
    softplus = jnp.maximum(xg, 0.0) + jnp.log1p(jnp.exp(-jnp.abs(xg)))
    g = -jnp.exp(alog_ref[...]) * softplus
    row = lax.broadcasted_iota(jnp.int32, (tb, tb), 0)
    col = lax.broadcasted_iota(jnp.int32, (tb, tb), 1)
    shift = int(math.log2(GDN_CHUNK))
    same = jnp.right_shift(row, shift) == jnp.right_shift(col, shift)
    incl = same & (col <= row)
    strict = same & (col < row)
    tri = jnp.where(incl, 1.0, 0.0).astype(BF16)
    g1, g2, g3 = _split3(g)
    gc = _dot(tri, g1) + (_dot(tri, g2) + _dot(tri, g3))
    gc_ref[...] = gc
    gct_ref[...] = gc.T

    lane = lax.broadcasted_iota(jnp.int32, (tb, LANES), 1)
    eye = jnp.where(row == col, 1.0, 0.0)

    def head(h, carry):
        c0 = pl.multiple_of(h * dk, dk)
        q = qc_ref[:, pl.ds(c0, dk)]
        k = kc_ref[:, pl.ds(c0, dk)]
        v = vc_ref[:, pl.ds(c0, dk)]
        q = q * lax.rsqrt(jnp.sum(q * q, axis=-1, keepdims=True) + EPS) * (dk ** -0.5)
        k = k * lax.rsqrt(jnp.sum(k * k, axis=-1, keepdims=True) + EPS)
        gcol = jnp.sum(jnp.where(lane == GDN_HEADS + h, gc_ref[...], 0.0), axis=-1, keepdims=True)
        bcol = jnp.sum(jnp.where(lane == h, beta_ref[...], 0.0), axis=-1, keepdims=True)
        grow = gct_ref[pl.ds(GDN_HEADS + h, 1), :]

        decay = jnp.where(incl, jnp.exp(jnp.where(incl, gcol - grow, 0.0)), 0.0)
        kb = k * bcol
        k16 = k.astype(BF16)
        kk = _dot_nt(kb.astype(BF16), k16)
        qk = _dot_nt(q.astype(BF16), k16)
        a_intra = (qk * decay).astype(BF16)
        m = -jnp.where(strict, kk * decay, 0.0)
        p = eye + m
        for _ in range(int(math.log2(GDN_CHUNK)) - 1):
            m16 = m.astype(BF16)
            m = _dot(m16, m16)
            p = p + _dot(p.astype(BF16), m.astype(BF16))
        eg = jnp.exp(gcol)
        rhs = jnp.concatenate([v * bcol, kb * eg], axis=1).astype(BF16)
        uw = _dot(p.astype(BF16), rhs)
        u = uw[:, :dk]
        w16 = uw[:, dk:].astype(BF16)
        qg16 = (q * eg).astype(BF16)

        vn_ref[...] = jnp.zeros_like(vn_ref)
        s = s_ref[h]
        for c in range(nchunk):
            r0, r1 = c * GDN_CHUNK, (c + 1) * GDN_CHUNK
            s16 = s.astype(BF16)
            v_new = u[r0:r1] - _dot(w16[r0:r1], s16)
            vn_ref[r0:r1, :] = v_new.astype(BF16)
            oh_ref[r0:r1, :] = _dot(qg16[r0:r1], s16) + _dot(a_intra[r0:r1], vn_ref[...])
            gl = gcol[r1 - 1:r1, :]
            kg = k[r0:r1] * jnp.exp(gl - gcol[r0:r1])
            s = s * jnp.exp(gl) + _dot_tn(kg.astype(BF16), v_new.astype(BF16))
        s_ref[h] = s

        o = oh_ref[...]
        z = z_ref[:, pl.ds(c0, dk)]
        o = _rms(o, nw_ref[...]) * (z * jax.nn.sigmoid(z))
        o_ref[:, pl.ds(c0, dk)] = o.astype(o_ref.dtype)
        return carry

    lax.fori_loop(0, GDN_HEADS, head, 0)


def _gdn(proj, ba, conv_w, a_log, dt_bias, norm_w, bsz, seq):
    tb = TB_GDN
    nt = seq // tb
    gw = GDN_WIDTH
    alog = jnp.zeros((1, LANES), F32).at[0, GDN_HEADS:2 * GDN_HEADS].set(a_log)
    dtb = jnp.zeros((1, LANES), F32).at[0, GDN_HEADS:2 * GDN_HEADS].set(dt_bias)

    def rows(b, t):
        return b * nt + t

    return pl.pallas_call(
        _gdn_kernel,
        out_shape=jax.ShapeDtypeStruct((bsz * seq, gw), BF16),
        grid=(bsz, nt),
        in_specs=[
            pl.BlockSpec((tb, gw), lambda b, t: (rows(b, t), 0)),
            pl.BlockSpec((tb, gw), lambda b, t: (rows(b, t), 1)),
            pl.BlockSpec((tb, gw), lambda b, t: (rows(b, t), 2)),
            pl.BlockSpec((tb, gw), lambda b, t: (rows(b, t), 3)),
            pl.BlockSpec((tb, LANES), lambda b, t: (rows(b, t), 0)),
            pl.BlockSpec((GDN_CONV, gw), lambda b, t: (0, 0)),
            pl.BlockSpec((GDN_CONV, gw), lambda b, t: (0, 1)),
            pl.BlockSpec((GDN_CONV, gw), lambda b, t: (0, 2)),
            pl.BlockSpec((1, LANES), lambda b, t: (0, 0)),
            pl.BlockSpec((1, LANES), lambda b, t: (0, 0)),
            pl.BlockSpec((1, GDN_HEAD_DIM), lambda b, t: (0, 0)),
        ],
        out_specs=pl.BlockSpec((tb, gw), lambda b, t: (rows(b, t), 0)),
        scratch_shapes=[
            pltpu.VMEM((CONV_HALO, gw), F32), pltpu.VMEM((CONV_HALO, gw), F32), pltpu.VMEM((CONV_HALO, gw), F32),
            pltpu.VMEM((CONV_HALO + tb, gw), F32),
            pltpu.VMEM((tb, gw), F32), pltpu.VMEM((tb, gw), F32), pltpu.VMEM((tb, gw), F32),
            pltpu.VMEM((tb, LANES), F32), pltpu.VMEM((LANES, tb), F32), pltpu.VMEM((tb, LANES), F32),
            pltpu.VMEM((GDN_HEADS, GDN_HEAD_DIM, GDN_HEAD_DIM), F32),
            pltpu.VMEM((tb, GDN_HEAD_DIM), BF16),
            pltpu.VMEM((tb, GDN_HEAD_DIM), F32),
        ],
        compiler_params=_cparams(("parallel", "arbitrary")),
        name="gdn",
    )(proj, proj, proj, proj, ba, conv_w, conv_w, conv_w, alog, dtb, norm_w.reshape(1, GDN_HEAD_DIM))


def _moba_prep_kernel(q_ref, k_ref, v_ref, qt_ref, k16_ref, vt_ref, km_ref):
    t = pl.program_id(1)
    qt_ref[...] = (q_ref[...] * (MOBA_HEAD_DIM ** -0.5)).T.astype(BF16)
    vt_ref[...] = v_ref[...].T.astype(BF16)
    k = k_ref[...]
    k16_ref[...] = k.astype(BF16)
    km_ref[pl.ds(t, 1), :] = jnp.mean(k, axis=0, keepdims=True)


def _moba_prep(proj, bsz, seq):
    blk = MOBA_BLOCK
    nb = seq // blk
    w = MOBA_WIDTH
    cb = COL_B // w
    return pl.pallas_call(
        _moba_prep_kernel,
        out_shape=(jax.ShapeDtypeStruct((bsz, w, seq), BF16),
                   jax.ShapeDtypeStruct((bsz * seq, w), BF16),
                   jax.ShapeDtypeStruct((bsz, w, seq), BF16),
                   jax.ShapeDtypeStruct((bsz, nb, w), F32)),
        grid=(bsz, nb),
        in_specs=[pl.BlockSpec((blk, w), lambda b, t: (b * nb + t, cb)),
                  pl.BlockSpec((blk, w), lambda b, t: (b * nb + t, cb + 1)),
                  pl.BlockSpec((blk, w), lambda b, t: (b * nb + t, cb + 2))],
        out_specs=(pl.BlockSpec((None, w, blk), lambda b, t: (b, 0, t)),
                   pl.BlockSpec((blk, w), lambda b, t: (b * nb + t, 0)),
                   pl.BlockSpec((None, w, blk), lambda b, t: (b, 0, t)),
                   pl.BlockSpec((None, nb, w), lambda b, t: (b, 0, 0))),
        compiler_params=_cparams(("parallel", "arbitrary")),
        name="moba_prep",
    )(proj, proj, proj)


def _moba_kernel(qi_ref, j_ref, qt_ref, k_ref, vt_ref, km_ref, bown_ref, bprev_ref, bfar_ref, o_ref,
                 m_ref, l_ref, acc_ref, msk_ref):
    step = pl.program_id(1)
    qi = qi_ref[step]
    j = j_ref[step]
    blk = MOBA_BLOCK
    hd = MOBA_HEAD_DIM
    nb = km_ref.shape[0]

    for h in range(MOBA_HEADS):
        rs = slice(h * hd, (h + 1) * hd)
        qt = qt_ref[rs, :]

        @pl.when(j == 0)
        def _():
            m_ref[h] = jnp.full((1, blk), NEG_INF, F32)
            l_ref[h] = jnp.zeros((1, blk), F32)
            acc_ref[h] = jnp.zeros((hd, blk), F32)
            k1, k2, _ = _split3(km_ref[:, rs])
            gate = _dot(k1, qt) + _dot(k2, qt)
            blkid = lax.broadcasted_iota(jnp.int32, (nb, blk), 0)
            gate = jnp.where(blkid < qi, gate, -jnp.inf)
            cnt = jnp.zeros((nb, blk), F32)
            for i in range(nb):
                gi = gate[i:i + 1, :]
                ahead = (gi > gate) | ((gi == gate) & (blkid > i))
                cnt = cnt + jnp.where(ahead, 1.0, 0.0)
            msk_ref[h] = jnp.where(cnt < MOBA_TOPK, 0.0, NEG_INF)

        s = _dot(k_ref[:, rs], qt)
        vt = vt_ref[rs, :]

        def update(s):
            m_prev = m_ref[h]
            m_new = jnp.maximum(m_prev, jnp.max(s, axis=0, keepdims=True))
            alpha = jnp.exp(m_prev - m_new)
            p = jnp.exp(s - m_new)
            l_ref[h] = alpha * l_ref[h] + jnp.sum(p, axis=0, keepdims=True)
            acc_ref[h] = alpha * acc_ref[h] + _dot(vt, p.astype(BF16))
            m_ref[h] = m_new

        @pl.when(j == 0)
        def _():
            update(s + bown_ref[h])

        @pl.when(j == 1)
        def _():
            update(s + bprev_ref[h] + msk_ref[h, pl.ds(qi - j, 1), :])

        @pl.when(j >= 2)
        def _():
            update(s + (bfar_ref[h] + msk_ref[h, pl.ds(qi - j, 1), :]))

        @pl.when(j == qi)
        def _():
            o_ref[:, rs] = (acc_ref[h] / l_ref[h]).T.astype(o_ref.dtype)


def _rel_bucket(dist):
    n = jnp.maximum(dist, 0)
    max_exact = REL_BUCKETS // 2
    nf = jnp.maximum(n, max_exact).astype(F32)
    large = max_exact + (jnp.log(nf / max_exact) / math.log(REL_MAX_DIST / max_exact)
                         * (REL_BUCKETS - max_exact)).astype(jnp.int32)
    large = jnp.minimum(large, REL_BUCKETS - 1)
    return jnp.where(n < max_exact, n, large)


def _bias_lookup(rel_bias, dist):
    bucket = _rel_bucket(dist)[None]
    out = jnp.zeros((rel_bias.shape[1],) + dist.shape, F32)
    for b in range(REL_BUCKETS):
        out = jnp.where(bucket == b, rel_bias[b].astype(F32).reshape((-1,) + (1,) * dist.ndim), out)
    return out


def _moba(qt, k16, vt, kmean, rel_bias, bsz, seq):
    blk = MOBA_BLOCK
    nb = seq // blk
    w = MOBA_WIDTH
    assert blk >= REL_MAX_DIST
    kk = jnp.arange(blk)[:, None]
    qq = jnp.arange(blk)[None, :]
    b_own = jnp.where((qq >= kk)[None], _bias_lookup(rel_bias, qq - kk), NEG_INF)
    b_prev = _bias_lookup(rel_bias, blk + qq - kk)
    b_far = jnp.broadcast_to(_bias_lookup(rel_bias, jnp.full((1, 1), 2 * blk)), (MOBA_HEADS, 1, blk))
    qi_list, j_list = [], []
    for qi in range(nb):
        for j in range(qi + 1):
            qi_list.append(qi)
            j_list.append(j)
    qi_arr = jnp.asarray(np.array(qi_list, np.int32))
    j_arr = jnp.asarray(np.array(j_list, np.int32))

    grid_spec = pltpu.PrefetchScalarGridSpec(
        num_scalar_prefetch=2,
        grid=(bsz, len(qi_list)),
        in_specs=[
            pl.BlockSpec((None, w, blk), lambda b, s, qi, j: (b, 0, qi[s])),
            pl.BlockSpec((blk, w), lambda b, s, qi, j: (b * nb + qi[s] - j[s], 0)),
            pl.BlockSpec((None, w, blk), lambda b, s, qi, j: (b, 0, qi[s] - j[s])),
            pl.BlockSpec((None, nb, w), lambda b, s, qi, j: (b, 0, 0)),
            pl.BlockSpec((MOBA_HEADS, blk, blk), lambda b, s, qi, j: (0, 0, 0)),
            pl.BlockSpec((MOBA_HEADS, blk, blk), lambda b, s, qi, j: (0, 0, 0)),
            pl.BlockSpec((MOBA_HEADS, 1, blk), lambda b, s, qi, j: (0, 0, 0)),
        ],
        out_specs=pl.BlockSpec((blk, w), lambda b, s, qi, j: (b * nb + qi[s], 0)),
        scratch_shapes=[
            pltpu.VMEM((MOBA_HEADS, 1, blk), F32),
            pltpu.VMEM((MOBA_HEADS, 1, blk), F32),
            pltpu.VMEM((MOBA_HEADS, MOBA_HEAD_DIM, blk), F32),
            pltpu.VMEM((MOBA_HEADS, nb, blk), F32),
        ],
    )
    return pl.pallas_call(
        _moba_kernel,
        out_shape=jax.ShapeDtypeStruct((bsz * seq, w), BF16),
        grid_spec=grid_spec,
        compiler_params=_cparams(("parallel", "arbitrary")),
        name="moba",
    )(qi_arr, j_arr, qt, k16, vt, kmean, b_own, b_prev, b_far)


def _sconv_kernel(x_ref, bg_ref, cg_ref, w_ref, o_ref, ext_ref):
    seq = x_ref.shape[0]
    u = cg_ref[...] * x_ref[...]
    ext_ref[0:CONV_HALO, :] = jnp.zeros((CONV_HALO, u.shape[1]), F32)
    ext_ref[CONV_HALO:, :] = u
    width = w_ref.shape[0]
    y = u * w_ref[width - 1:width, :]
    for s in range(1, width):
        y = y + ext_ref[CONV_HALO - s:CONV_HALO - s + seq, :] * w_ref[width - 1 - s:width - s, :]
    o_ref[...] = (bg_ref[...] * y).astype(o_ref.dtype)


def _sconv(proj, conv_w, bsz, seq):
    cw = SCONV_WIDTH
    nc = cw // LANES
    c0 = COL_C // LANES
    return pl.pallas_call(
        _sconv_kernel,
        out_shape=jax.ShapeDtypeStruct((bsz * seq, cw), BF16),
        grid=(bsz, nc),
        in_specs=[
            pl.BlockSpec((seq, LANES), lambda b, c: (b, c0 + c)),
            pl.BlockSpec((seq, LANES), lambda b, c: (b, c0 + nc + c)),
            pl.BlockSpec((seq, LANES), lambda b, c: (b, c0 + 2 * nc + c)),
            pl.BlockSpec((SCONV_K, LANES), lambda b, c: (0, c)),
        ],
        out_specs=pl.BlockSpec((seq, LANES), lambda b, c: (b, c)),
        scratch_shapes=[pltpu.VMEM((CONV_HALO + seq, LANES), F32)],
        compiler_params=_cparams(("parallel", "parallel")),
        name="sconv",
    )(proj, proj, proj, conv_w)


def _out_kernel(x_ref, oa_ref, ob_ref, oc_ref, w_ref, o_ref):
    a0, a1, a2 = GDN_WIDTH, GDN_WIDTH + MOBA_WIDTH, D_MODEL
    y = _dot(oa_ref[...], w_ref[0:a0, :])
    y = y + _dot(ob_ref[...], w_ref[a0:a1, :])
    y = y + _dot(oc_ref[...], w_ref[a1:a2, :])
    o_ref[...] = x_ref[...] + y


def _out(x, oa, ob, oc, w):
    n, d = x.shape
    return pl.pallas_call(
        _out_kernel,
        out_shape=jax.ShapeDtypeStruct((n, d), F32),
        grid=(n // TM_OUT,),
        in_specs=[
            pl.BlockSpec((TM_OUT, d), lambda i: (i, 0)),
            pl.BlockSpec((TM_OUT, GDN_WIDTH), lambda i: (i, 0)),
            pl.BlockSpec((TM_OUT, MOBA_WIDTH), lambda i: (i, 0)),
            pl.BlockSpec((TM_OUT, SCONV_WIDTH), lambda i: (i, 0)),
            pl.BlockSpec((d, d), lambda i: (0, 0)),
        ],
        out_specs=pl.BlockSpec((TM_OUT, d), lambda i: (i, 0)),
        compiler_params=_cparams(("parallel",)),
        name="out_proj",
    )(x, oa, ob, oc, w)


def _norm_kernel(x_ref, w_ref, o_ref):
    o_ref[...] = _rms(x_ref[...], w_ref[...])


def _final_norm(x, w):
    n, d = x.shape
    return pl.pallas_call(
        _norm_kernel,
        out_shape=jax.ShapeDtypeStruct((n, d), F32),
        grid=(n // TM_OUT,),
        in_specs=[pl.BlockSpec((TM_OUT, d), lambda i: (i, 0)), pl.BlockSpec((1, d), lambda i: (0, 0))],
        out_specs=pl.BlockSpec((TM_OUT, d), lambda i: (i, 0)),
        compiler_params=_cparams(("parallel",)),
        name="final_norm",
    )(x, w.reshape(1, d))


def _mixer(x, mix_norm, w_in, conv_qkv_w, a_log, dt_bias, gdn_norm_w, conv_c_w, w_out, rel_bias, bsz, seq):
    a_end = 4 * GDN_WIDTH
    ba_end = a_end + 2 * GDN_HEADS
    w_main = jnp.concatenate([w_in[:, :a_end], w_in[:, ba_end:]], axis=1).astype(BF16)
    w_ba = jnp.pad(w_in[:, a_end:ba_end], ((0, 0), (0, LANES - 2 * GDN_HEADS)))
    w_ba1 = w_ba.astype(BF16)
    w_ba2 = (w_ba - w_ba1.astype(F32)).astype(BF16)
    proj, ba = _proj(x, mix_norm, w_main, jnp.stack([w_ba1, w_ba2]))
    o_a = _gdn(proj, ba, conv_qkv_w, a_log, dt_bias, gdn_norm_w, bsz, seq)
    o_b = _moba(*_moba_prep(proj, bsz, seq), rel_bias, bsz, seq)
    o_c = _sconv(proj, conv_c_w, bsz, seq)
    return _out(x, o_a, o_b, o_c, w_out.astype(BF16))


def kernel(x, ffn1_norm, ffn1_w_gate, ffn1_w_up, ffn1_w_down, mix_norm, w_in, conv_qkv_w, a_log, dt_bias,
           gdn_norm_w, conv_c_w, w_out, ffn2_norm, ffn2_w_gate, ffn2_w_up, ffn2_w_down, rel_bias, final_norm):
    bsz, seq, d = x.shape
    x = x.reshape(bsz * seq, d)
    for l in range(DEPTH):
        x = _ffn(x, ffn1_norm[l], ffn1_w_gate[l].astype(BF16), ffn1_w_up[l].astype(BF16),
                 ffn1_w_down[l].astype(BF16))
        x = _mixer(x, mix_norm[l], w_in[l], conv_qkv_w[l], a_log[l], dt_bias[l], gdn_norm_w[l], conv_c_w[l],
                   w_out[l], rel_bias, bsz, seq)
        x = _ffn(x, ffn2_norm[l], ffn2_w_gate[l].astype(BF16), ffn2_w_up[l].astype(BF16),
                 ffn2_w_down[l].astype(BF16))
    return _final_norm(x, final_norm).reshape(bsz, seq, d)
```

```python
import functools
import math

import jax
import jax.numpy as jnp
import numpy as np
from jax import lax
from jax.experimental import pallas as pl
from jax.experimental.pallas import tpu as pltpu

F32 = jnp.float32
BF16 = jnp.bfloat16

D_MODEL = 2048
DEPTH = 2
EPS = 1e-6
D_FF = 5632

GDN_HEAD_DIM = 128
GDN_HEADS = 8
GDN_WIDTH = GDN_HEADS * GDN_HEAD_DIM
GDN_CONV = 4
GDN_CHUNK = 64

MOBA_HEAD_DIM = 128
MOBA_HEADS = 4
MOBA_WIDTH = MOBA_HEADS * MOBA_HEAD_DIM
MOBA_BLOCK = 256
MOBA_TOPK = 3

SCONV_WIDTH = D_MODEL - GDN_WIDTH - MOBA_WIDTH
SCONV_K = 3

REL_BUCKETS = 32
REL_MAX_DIST = 128
NEG_INF = -1e30

LANES = 128
CONV_HALO = 8

PROJ_MAIN = 4 * GDN_WIDTH + 3 * MOBA_WIDTH + 3 * SCONV_WIDTH
COL_B = 4 * GDN_WIDTH
COL_C = COL_B + 3 * MOBA_WIDTH

VMEM_LIMIT = 56 * 1024 * 1024

TM_FFN = 512
TF_FFN = 512
TM_PROJ = 512
TN_PROJ = 1024
TM_OUT = 512
TB_GDN = 256
HEADS_PER_TRIP = 4


def _cparams(sem):
    return pltpu.CompilerParams(dimension_semantics=sem, vmem_limit_bytes=VMEM_LIMIT)


def _rms(x, w):
    return x * lax.rsqrt(jnp.mean(x * x, axis=-1, keepdims=True) + EPS) * w


def _split3(x):
    x1 = x.astype(BF16)
    r1 = x - x1.astype(F32)
    x2 = r1.astype(BF16)
    x3 = (r1 - x2.astype(F32)).astype(BF16)
    return x1, x2, x3


def _dot(a, b):
    return jnp.dot(a, b, preferred_element_type=F32)


def _dot_nt(a, b):
    return lax.dot_general(a, b, (((1,), (1,)), ((), ())), preferred_element_type=F32)


def _dot_tn(a, b):
    return lax.dot_general(a, b, (((0,), (0,)), ((), ())), preferred_element_type=F32)


def _ffn_kernel(x_ref, nw_ref, wg_ref, wu_ref, wd_ref, o_ref, hn_ref, acc_ref, *, n_f):
    f = pl.program_id(1)

    @pl.when(f == 0)
    def _():
        hn_ref[...] = _rms(x_ref[...], nw_ref[...]).astype(BF16)
        acc_ref[...] = jnp.zeros_like(acc_ref)

    hn = hn_ref[...]
    g = _dot(hn, wg_ref[...])
    u = _dot(hn, wu_ref[...])
    a = (g * jax.nn.sigmoid(g) * u).astype(BF16)
    acc_ref[...] += _dot(a, wd_ref[...])

    @pl.when(f == n_f - 1)
    def _():
        o_ref[...] = x_ref[...] + 0.5 * acc_ref[...]


def _ffn(x, nw, wg, wu, wd):
    n, d = x.shape
    ff = wg.shape[1]
    n_f = ff // TF_FFN
    return pl.pallas_call(
        functools.partial(_ffn_kernel, n_f=n_f),
        out_shape=jax.ShapeDtypeStruct((n, d), F32),
        grid=(n // TM_FFN, n_f),
        in_specs=[
            pl.BlockSpec((TM_FFN, d), lambda i, f: (i, 0)),
            pl.BlockSpec((1, d), lambda i, f: (0, 0)),
            pl.BlockSpec((d, TF_FFN), lambda i, f: (0, f)),
            pl.BlockSpec((d, TF_FFN), lambda i, f: (0, f)),
            pl.BlockSpec((TF_FFN, d), lambda i, f: (f, 0)),
        ],
        out_specs=pl.BlockSpec((TM_FFN, d), lambda i, f: (i, 0)),
        scratch_shapes=[pltpu.VMEM((TM_FFN, d), BF16), pltpu.VMEM((TM_FFN, d), F32)],
        compiler_params=_cparams(("parallel", "arbitrary")),
        name="ffn",
    )(x, nw.reshape(1, d), wg, wu, wd)


def _proj_kernel(x_ref, nw_ref, w_ref, wba_ref, o_ref, ba_ref, hn_ref):
    n = pl.program_id(1)

    @pl.when(n == 0)
    def _():
        h = _rms(x_ref[...], nw_ref[...])
        h1, h2, _ = _split3(h)
        hn_ref[...] = h1
        w1 = wba_ref[0]
        w2 = wba_ref[1]
        ba_ref[...] = _dot(h1, w1) + (_dot(h1, w2) + _dot(h2, w1))

    o_ref[...] = _dot(hn_ref[...], w_ref[...])


def _proj(x, nw, w_main, w_ba):
    n, d = x.shape
    nm = w_main.shape[1]
    return pl.pallas_call(
        _proj_kernel,
        out_shape=(jax.ShapeDtypeStruct((n, nm), F32), jax.ShapeDtypeStruct((n, LANES), F32)),
        grid=(n // TM_PROJ, nm // TN_PROJ),
        in_specs=[
            pl.BlockSpec((TM_PROJ, d), lambda i, j: (i, 0)),
            pl.BlockSpec((1, d), lambda i, j: (0, 0)),
            pl.BlockSpec((d, TN_PROJ), lambda i, j: (0, j)),
            pl.BlockSpec((2, d, LANES), lambda i, j: (0, 0, 0)),
        ],
        out_specs=(pl.BlockSpec((TM_PROJ, TN_PROJ), lambda i, j: (i, j)),
                   pl.BlockSpec((TM_PROJ, LANES), lambda i, j: (i, 0))),
        scratch_shapes=[pltpu.VMEM((TM_PROJ, d), BF16)],
        compiler_params=_cparams(("parallel", "arbitrary")),
        name="proj",
    )(x, nw.reshape(1, d), w_main, w_ba)


def _conv_silu(x_ref, hist_ref, ext_ref, w_ref, first):
    tb = x_ref.shape[0]

    @pl.when(first)
    def _():
        hist_ref[...] = jnp.zeros_like(hist_ref)

    x = x_ref[...]
    ext_ref[0:CONV_HALO, :] = hist_ref[...]
    ext_ref[CONV_HALO:, :] = x
    hist_ref[...] = x[tb - CONV_HALO:, :]
    width = w_ref.shape[0]
    y = x * w_ref[width - 1:width, :]
    for s in range(1, width):
        y = y + ext_ref[CONV_HALO - s:CONV_HALO - s + tb, :] * w_ref[width - 1 - s:width - s, :]
    return y * jax.nn.sigmoid(y)


def _gdn_kernel(q_ref, k_ref, v_ref, z_ref, ba_ref, wq_ref, wk_ref, wv_ref, alog_ref, dtb_ref, nw_ref,
                o_ref,
                hq_ref, hk_ref, hv_ref, ext_ref, qc_ref, kc_ref, vc_ref, gc_ref, gct_ref, beta_ref,
                s_ref):
    t = pl.program_id(1)
    tb = q_ref.shape[0]
    nchunk = tb // GDN_CHUNK
    dk = GDN_HEAD_DIM
    first = t == 0

    qc_ref[...] = _conv_silu(q_ref, hq_ref, ext_ref, wq_ref, first)
    kc_ref[...] = _conv_silu(k_ref, hk_ref, ext_ref, wk_ref, first)
    vc_ref[...] = _conv_silu(v_ref, hv_ref, ext_ref, wv_ref, first)

    @pl.when(first)
    def _():
        s_ref[...] = jnp.zeros_like(s_ref)

    ba = ba_ref[...]
    beta_ref[...] = jax.nn.sigmoid(ba)
    xg = ba + dtb_ref[...]
    softplus = jnp.maximum(xg, 0.0) + jnp.log1p(jnp.exp(-jnp.abs(xg)))
    g = -jnp.exp(alog_ref[...]) * softplus
    row = lax.broadcasted_iota(jnp.int32, (tb, tb), 0)
    col = lax.broadcasted_iota(jnp.int32, (tb, tb), 1)
    shift = int(math.log2(GDN_CHUNK))
    same = jnp.right_shift(row, shift) == jnp.right_shift(col, shift)
    incl = same & (col <= row)
    strict = same & (col < row)
    tri = jnp.where(incl, 1.0, 0.0).astype(BF16)
    g1, g2, g3 = _split3(g)
    gc = _dot(tri, g1) + (_dot(tri, g2) + _dot(tri, g3))
    gc_ref[...] = gc
    gct_ref[...] = gc.T

    lane = lax.broadcasted_iota(jnp.int32, (tb, LANES), 1)
    eye = jnp.where(row == col, 1.0, 0.0)

    def head(h, s, done):
        c0 = pl.multiple_of(h * dk, dk)
        q = qc_ref[:, pl.ds(c0, dk)]
        k = kc_ref[:, pl.ds(c0, dk)]
        v = vc_ref[:, pl.ds(c0, dk)]
        z = z_ref[:, pl.ds(c0, dk)]
        q = q * lax.rsqrt(jnp.sum(q * q, axis=-1, keepdims=True) + EPS) * (dk ** -0.5)
        k = k * lax.rsqrt(jnp.sum(k * k, axis=-1, keepdims=True) + EPS)
        gcol = jnp.sum(jnp.where(lane == GDN_HEADS + h, gc_ref[...], 0.0), axis=-1, keepdims=True)
        bcol = jnp.sum(jnp.where(lane == h, beta_ref[...], 0.0), axis=-1, keepdims=True)
        grow = gct_ref[pl.ds(GDN_HEADS + h, 1), :]

        decay = jnp.where(incl, jnp.exp(jnp.where(incl, gcol - grow, 0.0)), 0.0)
        kb = k * bcol
        k16 = k.astype(BF16)
        kk = _dot_nt(kb.astype(BF16), k16)
        qk = _dot_nt(q.astype(BF16), k16)
        yield
        a_intra = (qk * decay).astype(BF16)
        m = -jnp.where(strict, kk * decay, 0.0)
        p = eye + m
        for _ in range(int(math.log2(GDN_CHUNK)) - 1):
            m16 = m.astype(BF16)
            m = _dot(m16, m16)
            yield
            p = p + _dot(p.astype(BF16), m.astype(BF16))
        eg = jnp.exp(gcol)
        rhs = jnp.concatenate([v * bcol, kb * eg], axis=1).astype(BF16)
        yield
        uw = _dot(p.astype(BF16), rhs)
        yield
        u = uw[:, :dk]
        w16 = uw[:, dk:].astype(BF16)
        qg16 = (q * eg).astype(BF16)

        zero16 = jnp.zeros((GDN_CHUNK, dk), BF16)
        outs = []
        for c in range(nchunk):
            r0, r1 = c * GDN_CHUNK, (c + 1) * GDN_CHUNK
            s16 = s.astype(BF16)
            ws = _dot(w16[r0:r1], s16)
            qs = _dot(qg16[r0:r1], s16)
            yield
            v_new = (u[r0:r1] - ws).astype(BF16)
            vn_full = jnp.concatenate([zero16] * c + [v_new] + [zero16] * (nchunk - 1 - c), axis=0)
            gl = gcol[r1 - 1:r1, :]
            kg = k[r0:r1] * jnp.exp(gl - gcol[r0:r1])
            s = s * jnp.exp(gl) + _dot_tn(kg.astype(BF16), v_new)
            outs.append(qs + _dot(a_intra[r0:r1], vn_full))
            yield
        o = jnp.concatenate(outs, axis=0)
        o = _rms(o, nw_ref[...]) * (z * jax.nn.sigmoid(z))
        done.append((o.astype(o_ref.dtype), s))

    def head_group(i, carry):
        hs = [i * HEADS_PER_TRIP + u for u in range(HEADS_PER_TRIP)]
        results = []
        live = [head(h, s_ref[h], results) for h in hs]
        while live:
            live = [g for g in live if next(g, live) is not live]
        for h, (o, s) in zip(hs, results):
            s_ref[h] = s
            o_ref[:, pl.ds(pl.multiple_of(h * dk, dk), dk)] = o
        return carry

    lax.fori_loop(0, GDN_HEADS // HEADS_PER_TRIP, head_group, 0)


def _gdn(proj, ba, conv_w, a_log, dt_bias, norm_w, bsz, seq):
    tb = TB_GDN
    nt = seq // tb
    gw = GDN_WIDTH
    alog = jnp.zeros((1, LANES), F32).at[0, GDN_HEADS:2 * GDN_HEADS].set(a_log)
    dtb = jnp.zeros((1, LANES), F32).at[0, GDN_HEADS:2 * GDN_HEADS].set(dt_bias)

    def rows(b, t):
        return b * nt + t

    return pl.pallas_call(
        _gdn_kernel,
        out_shape=jax.ShapeDtypeStruct((bsz * seq, gw), BF16),
        grid=(bsz, nt),
        in_specs=[
            pl.BlockSpec((tb, gw), lambda b, t: (rows(b, t), 0)),
            pl.BlockSpec((tb, gw), lambda b, t: (rows(b, t), 1)),
            pl.BlockSpec((tb, gw), lambda b, t: (rows(b, t), 2)),
            pl.BlockSpec((tb, gw), lambda b, t: (rows(b, t), 3)),
            pl.BlockSpec((tb, LANES), lambda b, t: (rows(b, t), 0)),
            pl.BlockSpec((GDN_CONV, gw), lambda b, t: (0, 0)),
            pl.BlockSpec((GDN_CONV, gw), lambda b, t: (0, 1)),
            pl.BlockSpec((GDN_CONV, gw), lambda b, t: (0, 2)),
            pl.BlockSpec((1, LANES), lambda b, t: (0, 0)),
            pl.BlockSpec((1, LANES), lambda b, t: (0, 0)),
            pl.BlockSpec((1, GDN_HEAD_DIM), lambda b, t: (0, 0)),
        ],
        out_specs=pl.BlockSpec((tb, gw), lambda b, t: (rows(b, t), 0)),
        scratch_shapes=[
            pltpu.VMEM((CONV_HALO, gw), F32), pltpu.VMEM((CONV_HALO, gw), F32), pltpu.VMEM((CONV_HALO, gw), F32),
            pltpu.VMEM((CONV_HALO + tb, gw), F32),
            pltpu.VMEM((tb, gw), F32), pltpu.VMEM((tb, gw), F32), pltpu.VMEM((tb, gw), F32),
            pltpu.VMEM((tb, LANES), F32), pltpu.VMEM((LANES, tb), F32), pltpu.VMEM((tb, LANES), F32),
            pltpu.VMEM((GDN_HEADS, GDN_HEAD_DIM, GDN_HEAD_DIM), F32),
        ],
        compiler_params=_cparams(("parallel", "arbitrary")),
        name="gdn",
    )(proj, proj, proj, proj, ba, conv_w, conv_w, conv_w, alog, dtb, norm_w.reshape(1, GDN_HEAD_DIM))


def _moba_prep_kernel(q_ref, k_ref, v_ref, qt_ref, k16_ref, vt_ref, km_ref):
    t = pl.program_id(1)
    qt_ref[...] = (q_ref[...] * (MOBA_HEAD_DIM ** -0.5)).T.astype(BF16)
    vt_ref[...] = v_ref[...].T.astype(BF16)
    k = k_ref[...]
    k16_ref[...] = k.astype(BF16)
    km_ref[pl.ds(t, 1), :] = jnp.mean(k, axis=0, keepdims=True)


def _moba_prep(proj, bsz, seq):
    blk = MOBA_BLOCK
    nb = seq // blk
    w = MOBA_WIDTH
    cb = COL_B // w
    return pl.pallas_call(
        _moba_prep_kernel,
        out_shape=(jax.ShapeDtypeStruct((bsz, w, seq), BF16),
                   jax.ShapeDtypeStruct((bsz * seq, w), BF16),
                   jax.ShapeDtypeStruct((bsz, w, seq), BF16),
                   jax.ShapeDtypeStruct((bsz, nb, w), F32)),
        grid=(bsz, nb),
        in_specs=[pl.BlockSpec((blk, w), lambda b, t: (b * nb + t, cb)),
                  pl.BlockSpec((blk, w), lambda b, t: (b * nb + t, cb + 1)),
                  pl.BlockSpec((blk, w), lambda b, t: (b * nb + t, cb + 2))],
        out_specs=(pl.BlockSpec((None, w, blk), lambda b, t: (b, 0, t)),
                   pl.BlockSpec((blk, w), lambda b, t: (b * nb + t, 0)),
                   pl.BlockSpec((None, w, blk), lambda b, t: (b, 0, t)),
                   pl.BlockSpec((None, nb, w), lambda b, t: (b, 0, 0))),
        compiler_params=_cparams(("parallel", "arbitrary")),
        name="moba_prep",
    )(proj, proj, proj)


def _moba_kernel(qi_ref, j_ref, qt_ref, k_ref, vt_ref, km_ref, bown_ref, bprev_ref, bfar_ref, o_ref,
                 m_ref, l_ref, acc_ref, msk_ref):
    step = pl.program_id(1)
    qi = qi_ref[step]
    j = j_ref[step]
    blk = MOBA_BLOCK
    hd = MOBA_HEAD_DIM
    nb = km_ref.shape[0]

    def rows(h):
        return slice(h * hd, (h + 1) * hd)

    def update(bias):
        heads = range(MOBA_HEADS)
        old = [(m_ref[h], l_ref[h], acc_ref[h]) for h in heads]
        scores = [_dot(k_ref[:, rows(h)], qt_ref[rows(h), :]) for h in heads]
        new = []
        for h in heads:
            m_prev, l_prev, acc_prev = old[h]
            s = scores[h] + bias(h)
            m_new = jnp.maximum(m_prev, jnp.max(s, axis=0, keepdims=True))
            alpha = jnp.exp(m_prev - m_new)
            p = jnp.exp(s - m_new)
            new.append((m_new, alpha * l_prev + jnp.sum(p, axis=0, keepdims=True),
                        alpha * acc_prev + _dot(vt_ref[rows(h), :], p.astype(BF16))))
        for h in heads:
            m_ref[h], l_ref[h], acc_ref[h] = new[h]

    @pl.when(j == 0)
    def _():
        blkid = lax.broadcasted_iota(jnp.int32, (nb, blk), 0)
        for h in range(MOBA_HEADS):
            m_ref[h] = jnp.full((1, blk), NEG_INF, F32)
            l_ref[h] = jnp.zeros((1, blk), F32)
            acc_ref[h] = jnp.zeros((hd, blk), F32)
            k1, k2, _ = _split3(km_ref[:, rows(h)])
            qt = qt_ref[rows(h), :]
            gate = _dot(k1, qt) + _dot(k2, qt)
            gate = jnp.where(blkid < qi, gate, -jnp.inf)
            cnt = jnp.zeros((nb, blk), F32)
            for i in range(nb):
                gi = gate[i:i + 1, :]
                ahead = (gi > gate) | ((gi == gate) & (blkid > i))
                cnt = cnt + jnp.where(ahead, 1.0, 0.0)
            msk_ref[h] = jnp.where(cnt < MOBA_TOPK, 0.0, NEG_INF)
        update(lambda h: bown_ref[h])

    @pl.when(j == 1)
    def _():
        update(lambda h: bprev_ref[h] + msk_ref[h, pl.ds(qi - j, 1), :])

    @pl.when(j >= 2)
    def _():
        update(lambda h: bfar_ref[h] + msk_ref[h, pl.ds(qi - j, 1), :])

    @pl.when(j == qi)
    def _():
        for h in range(MOBA_HEADS):
            o_ref[:, rows(h)] = (acc_ref[h] / l_ref[h]).T.astype(o_ref.dtype)


def _rel_bucket(dist):
    n = jnp.maximum(dist, 0)
    max_exact = REL_BUCKETS // 2
    nf = jnp.maximum(n, max_exact).astype(F32)
    large = max_exact + (jnp.log(nf / max_exact) / math.log(REL_MAX_DIST / max_exact)
                         * (REL_BUCKETS - max_exact)).astype(jnp.int32)
    large = jnp.minimum(large, REL_BUCKETS - 1)
    return jnp.where(n < max_exact, n, large)


def _bias_lookup(rel_bias, dist):
    bucket = _rel_bucket(dist)[None]
    out = jnp.zeros((rel_bias.shape[1],) + dist.shape, F32)
    for b in range(REL_BUCKETS):
        out = jnp.where(bucket == b, rel_bias[b].astype(F32).reshape((-1,) + (1,) * dist.ndim), out)
    return out


def _moba(qt, k16, vt, kmean, rel_bias, bsz, seq):
    blk = MOBA_BLOCK
    nb = seq // blk
    w = MOBA_WIDTH
    assert blk >= REL_MAX_DIST
    kk = jnp.arange(blk)[:, None]
    qq = jnp.arange(blk)[None, :]
    b_own = jnp.where((qq >= kk)[None], _bias_lookup(rel_bias, qq - kk), NEG_INF)
    b_prev = _bias_lookup(rel_bias, blk + qq - kk)
    b_far = jnp.broadcast_to(_bias_lookup(rel_bias, jnp.full((1, 1), 2 * blk)), (MOBA_HEADS, 1, blk))
    qi_list, j_list = [], []
    for qi in range(nb):
        for j in range(qi + 1):
            qi_list.append(qi)
            j_list.append(j)
    qi_arr = jnp.asarray(np.array(qi_list, np.int32))
    j_arr = jnp.asarray(np.array(j_list, np.int32))

    grid_spec = pltpu.PrefetchScalarGridSpec(
        num_scalar_prefetch=2,
        grid=(bsz, len(qi_list)),
        in_specs=[
            pl.BlockSpec((None, w, blk), lambda b, s, qi, j: (b, 0, qi[s])),
            pl.BlockSpec((blk, w), lambda b, s, qi, j: (b * nb + qi[s] - j[s], 0)),
            pl.BlockSpec((None, w, blk), lambda b, s, qi, j: (b, 0, qi[s] - j[s])),
            pl.BlockSpec((None, nb, w), lambda b, s, qi, j: (b, 0, 0)),
            pl.BlockSpec((MOBA_HEADS, blk, blk), lambda b, s, qi, j: (0, 0, 0)),
            pl.BlockSpec((MOBA_HEADS, blk, blk), lambda b, s, qi, j: (0, 0, 0)),
            pl.BlockSpec((MOBA_HEADS, 1, blk), lambda b, s, qi, j: (0, 0, 0)),
        ],
        out_specs=pl.BlockSpec((blk, w), lambda b, s, qi, j: (b * nb + qi[s], 0)),
        scratch_shapes=[
            pltpu.VMEM((MOBA_HEADS, 1, blk), F32),
            pltpu.VMEM((MOBA_HEADS, 1, blk), F32),
            pltpu.VMEM((MOBA_HEADS, MOBA_HEAD_DIM, blk), F32),
            pltpu.VMEM((MOBA_HEADS, nb, blk), F32),
        ],
    )
    return pl.pallas_call(
        _moba_kernel,
        out_shape=jax.ShapeDtypeStruct((bsz * seq, w), BF16),
        grid_spec=grid_spec,
        compiler_params=_cparams(("parallel", "arbitrary")),
        name="moba",
    )(qi_arr, j_arr, qt, k16, vt, kmean, b_own, b_prev, b_far)


def _sconv_kernel(x_ref, bg_ref, cg_ref, w_ref, o_ref, ext_ref):
    seq = x_ref.shape[0]
    u = cg_ref[...] * x_ref[...]
    ext_ref[0:CONV_HALO, :] = jnp.zeros((CONV_HALO, u.shape[1]), F32)
    ext_ref[CONV_HALO:, :] = u
    width = w_ref.shape[0]
    y = u * w_ref[width - 1:width, :]
    for s in range(1, width):
        y = y + ext_ref[CONV_HALO - s:CONV_HALO - s + seq, :] * w_ref[width - 1 - s:width - s, :]
    o_ref[...] = (bg_ref[...] * y).astype(o_ref.dtype)


def _sconv(proj, conv_w, bsz, seq):
    cw = SCONV_WIDTH
    nc = cw // LANES
    c0 = COL_C // LANES
    return pl.pallas_call(
        _sconv_kernel,
        out_shape=jax.ShapeDtypeStruct((bsz * seq, cw), BF16),
        grid=(bsz, nc),
        in_specs=[
            pl.BlockSpec((seq, LANES), lambda b, c: (b, c0 + c)),
            pl.BlockSpec((seq, LANES), lambda b, c: (b, c0 + nc + c)),
            pl.BlockSpec((seq, LANES), lambda b, c: (b, c0 + 2 * nc + c)),
            pl.BlockSpec((SCONV_K, LANES), lambda b, c: (0, c)),
        ],
        out_specs=pl.BlockSpec((seq, LANES), lambda b, c: (b, c)),
        scratch_shapes=[pltpu.VMEM((CONV_HALO + seq, LANES), F32)],
        compiler_params=_cparams(("parallel", "parallel")),
        name="sconv",
    )(proj, proj, proj, conv_w)


def _out_kernel(x_ref, oa_ref, ob_ref, oc_ref, w_ref, o_ref):
    a0, a1, a2 = GDN_WIDTH, GDN_WIDTH + MOBA_WIDTH, D_MODEL
    y = _dot(oa_ref[...], w_ref[0:a0, :])
    y = y + _dot(ob_ref[...], w_ref[a0:a1, :])
    y = y + _dot(oc_ref[...], w_ref[a1:a2, :])
    o_ref[...] = x_ref[...] + y


def _out(x, oa, ob, oc, w):
    n, d = x.shape
    return pl.pallas_call(
        _out_kernel,
        out_shape=jax.ShapeDtypeStruct((n, d), F32),
        grid=(n // TM_OUT,),
        in_specs=[
            pl.BlockSpec((TM_OUT, d), lambda i: (i, 0)),
            pl.BlockSpec((TM_OUT, GDN_WIDTH), lambda i: (i, 0)),
            pl.BlockSpec((TM_OUT, MOBA_WIDTH), lambda i: (i, 0)),
            pl.BlockSpec((TM_OUT, SCONV_WIDTH), lambda i: (i, 0)),
            pl.BlockSpec((d, d), lambda i: (0, 0)),
        ],
        out_specs=pl.BlockSpec((TM_OUT, d), lambda i: (i, 0)),
        compiler_params=_cparams(("parallel",)),
        name="out_proj",
    )(x, oa, ob, oc, w)


def _norm_kernel(x_ref, w_ref, o_ref):
    o_ref[...] = _rms(x_ref[...], w_ref[...])


def _final_norm(x, w):
    n, d = x.shape
    return pl.pallas_call(
        _norm_kernel,
        out_shape=jax.ShapeDtypeStruct((n, d), F32),
        grid=(n // TM_OUT,),
        in_specs=[pl.BlockSpec((TM_OUT, d), lambda i: (i, 0)), pl.BlockSpec((1, d), lambda i: (0, 0))],
        out_specs=pl.BlockSpec((TM_OUT, d), lambda i: (i, 0)),
        compiler_params=_cparams(("parallel",)),
        name="final_norm",
    )(x, w.reshape(1, d))


def _mixer(x, mix_norm, w_in, conv_qkv_w, a_log, dt_bias, gdn_norm_w, conv_c_w, w_out, rel_bias, bsz, seq):
    a_end = 4 * GDN_WIDTH
    ba_end = a_end + 2 * GDN_HEADS
    w_main = jnp.concatenate([w_in[:, :a_end], w_in[:, ba_end:]], axis=1).astype(BF16)
    w_ba = jnp.pad(w_in[:, a_end:ba_end], ((0, 0), (0, LANES - 2 * GDN_HEADS)))
    w_ba1 = w_ba.astype(BF16)
    w_ba2 = (w_ba - w_ba1.astype(F32)).astype(BF16)
    proj, ba = _proj(x, mix_norm, w_main, jnp.stack([w_ba1, w_ba2]))
    o_a = _gdn(proj, ba, conv_qkv_w, a_log, dt_bias, gdn_norm_w, bsz, seq)
    o_b = _moba(*_moba_prep(proj, bsz, seq), rel_bias, bsz, seq)
    o_c = _sconv(proj, conv_c_w, bsz, seq)
    return _out(x, o_a, o_b, o_c, w_out.astype(BF16))


def kernel(x, ffn1_norm, ffn1_w_gate, ffn1_w_up, ffn1_w_down, mix_norm, w_in, conv_qkv_w, a_log, dt_bias,
           gdn_norm_w, conv_c_w, w_out, ffn2_norm, ffn2_w_gate, ffn2_w_up, ffn2_w_down, rel_bias, final_norm):
    bsz, seq, d = x.shape
    x = x.reshape(bsz * seq, d)
    for l in range(DEPTH):
        x = _ffn(x, ffn1_norm[l], ffn1_w_gate[l].astype(BF16), ffn1_w_up[l].astype(BF16),
                 ffn1_w_down[l].astype(BF16))
        x = _mixer(x, mix_norm[l], w_in[l], conv_qkv_w[l], a_log[l], dt_bias[l], gdn_norm_w[l], conv_c_w[l],
                   w_out[l], rel_bias, bsz, seq)
        x = _ffn(x, ffn2_norm[l], ffn2_w_gate[l].astype(BF16), ffn2_w_up[l].astype(BF16),
                 ffn2_w_down[l].astype(BF16))
    return _final_norm(x, final_norm).reshape(bsz, seq, d)
```

```python
import functools
import math

import jax
import jax.numpy as jnp
import numpy as np
from jax import lax
from jax.experimental import pallas as pl
from jax.experimental.pallas import tpu as pltpu

F32 = jnp.float32
BF16 = jnp.bfloat16

D_MODEL = 2048
DEPTH = 2
EPS = 1e-6
D_FF = 5632

GDN_HEAD_DIM = 128
GDN_HEADS = 8
GDN_WIDTH = GDN_HEADS * GDN_HEAD_DIM
GDN_CONV = 4
GDN_CHUNK = 64

MOBA_HEAD_DIM = 128
MOBA_HEADS = 4
MOBA_WIDTH = MOBA_HEADS * MOBA_HEAD_DIM
MOBA_BLOCK = 256
MOBA_TOPK = 3

SCONV_WIDTH = D_MODEL - GDN_WIDTH - MOBA_WIDTH
SCONV_K = 3

REL_BUCKETS = 32
REL_MAX_DIST = 128
NEG_INF = -1e30

LANES = 128
CONV_HALO = 8

PROJ_MAIN = 4 * GDN_WIDTH + 3 * MOBA_WIDTH + 3 * SCONV_WIDTH
COL_B = 4 * GDN_WIDTH
COL_C = COL_B + 3 * MOBA_WIDTH

VMEM_LIMIT = 56 * 1024 * 1024

TM_FFN = 512
TF_FFN = 512
TM_PROJ = 512
TN_PROJ = 1024
TM_OUT = 512
TB_GDN = 256
HEADS_PER_TRIP = 8


def _cparams(sem):
    return pltpu.CompilerParams(dimension_semantics=sem, vmem_limit_bytes=VMEM_LIMIT)


def _rms(x, w):
    return x * lax.rsqrt(jnp.mean(x * x, axis=-1, keepdims=True) + EPS) * w


def _split3(x):
    x1 = x.astype(BF16)
    r1 = x - x1.astype(F32)
    x2 = r1.astype(BF16)
    x3 = (r1 - x2.astype(F32)).astype(BF16)
    return x1, x2, x3


def _dot(a, b):
    return jnp.dot(a, b, preferred_element_type=F32)


def _dot_nt(a, b):
    return lax.dot_general(a, b, (((1,), (1,)), ((), ())), preferred_element_type=F32)


def _dot_tn(a, b):
    return lax.dot_general(a, b, (((0,), (0,)), ((), ())), preferred_element_type=F32)


def _ffn_kernel(x_ref, nw_ref, wg_ref, wu_ref, wd_ref, fw_ref, o_ref, hn_ref, acc_ref, *, n_f, final_norm):
    f = pl.program_id(1)

    @pl.when(f == 0)
    def _():
        hn_ref[...] = _rms(x_ref[...], nw_ref[...]).astype(BF16)
        acc_ref[...] = jnp.zeros_like(acc_ref)

    hn = hn_ref[...]
    g = _dot(hn, wg_ref[...])
    u = _dot(hn, wu_ref[...])
    a = (g * jax.nn.sigmoid(g) * u).astype(BF16)
    acc_ref[...] += _dot(a, wd_ref[...])

    @pl.when(f == n_f - 1)
    def _():
        y = x_ref[...] + 0.5 * acc_ref[...]
        o_ref[...] = _rms(y, fw_ref[...]) if final_norm else y


def _ffn(x, l, nw, wg, wu, wd, fw, final_norm):
    n, d = x.shape
    ff = wg.shape[2]
    n_f = ff // TF_FFN
    return pl.pallas_call(
        functools.partial(_ffn_kernel, n_f=n_f, final_norm=final_norm),
        out_shape=jax.ShapeDtypeStruct((n, d), F32),
        grid=(n // TM_FFN, n_f),
        in_specs=[
            pl.BlockSpec((TM_FFN, d), lambda i, f: (i, 0)),
            pl.BlockSpec((1, d), lambda i, f: (0, 0)),
            pl.BlockSpec((None, d, TF_FFN), lambda i, f: (l, 0, f)),
            pl.BlockSpec((None, d, TF_FFN), lambda i, f: (l, 0, f)),
            pl.BlockSpec((None, TF_FFN, d), lambda i, f: (l, f, 0)),
            pl.BlockSpec((1, d), lambda i, f: (0, 0)),
        ],
        out_specs=pl.BlockSpec((TM_FFN, d), lambda i, f: (i, 0)),
        scratch_shapes=[pltpu.VMEM((TM_FFN, d), BF16), pltpu.VMEM((TM_FFN, d), F32)],
        compiler_params=_cparams(("parallel", "arbitrary")),
        name="ffn",
    )(x, nw.reshape(1, d), wg, wu, wd, fw.reshape(1, d))


def _proj_kernel(x_ref, nw_ref, w_ref, wba_ref, o_ref, ba_ref, hn_ref):
    n = pl.program_id(1)

    @pl.when(n == 0)
    def _():
        h = _rms(x_ref[...], nw_ref[...])
        h1, h2, _ = _split3(h)
        hn_ref[...] = h1
        w1 = wba_ref[0]
        w2 = wba_ref[1]
        ba_ref[...] = _dot(h1, w1) + (_dot(h1, w2) + _dot(h2, w1))

    o_ref[...] = _dot(hn_ref[...], w_ref[...])


def _proj(x, l, nw, w_main, w_ba):
    n, d = x.shape
    nm = w_main.shape[2]
    return pl.pallas_call(
        _proj_kernel,
        out_shape=(jax.ShapeDtypeStruct((n, nm), F32), jax.ShapeDtypeStruct((n, LANES), F32)),
        grid=(n // TM_PROJ, nm // TN_PROJ),
        in_specs=[
            pl.BlockSpec((TM_PROJ, d), lambda i, j: (i, 0)),
            pl.BlockSpec((1, d), lambda i, j: (0, 0)),
            pl.BlockSpec((None, d, TN_PROJ), lambda i, j: (l, 0, j)),
            pl.BlockSpec((None, 2, d, LANES), lambda i, j: (l, 0, 0, 0)),
        ],
        out_specs=(pl.BlockSpec((TM_PROJ, TN_PROJ), lambda i, j: (i, j)),
                   pl.BlockSpec((TM_PROJ, LANES), lambda i, j: (i, 0))),
        scratch_shapes=[pltpu.VMEM((TM_PROJ, d), BF16)],
        compiler_params=_cparams(("parallel", "arbitrary")),
        name="proj",
    )(x, nw.reshape(1, d), w_main, w_ba)


def _conv_silu(x_ref, hist_ref, ext_ref, w_ref, first):
    tb = x_ref.shape[0]

    @pl.when(first)
    def _():
        hist_ref[...] = jnp.zeros_like(hist_ref)

    x = x_ref[...]
    ext_ref[0:CONV_HALO, :] = hist_ref[...]
    ext_ref[CONV_HALO:, :] = x
    hist_ref[...] = x[tb - CONV_HALO:, :]
    width = w_ref.shape[0]
    y = x * w_ref[width - 1:width, :]
    for s in range(1, width):
        y = y + ext_ref[CONV_HALO - s:CONV_HALO - s + tb, :] * w_ref[width - 1 - s:width - s, :]
    return y * jax.nn.sigmoid(y)


def _gdn_kernel(q_ref, k_ref, v_ref, z_ref, ba_ref, wq_ref, wk_ref, wv_ref, alog_ref, dtb_ref, nw_ref,
                o_ref,
                hq_ref, hk_ref, hv_ref, ext_ref, qc_ref, kc_ref, vc_ref, gc_ref, gct_ref, beta_ref,
                s_ref):
    t = pl.program_id(1)
    tb = q_ref.shape[0]
    nchunk = tb // GDN_CHUNK
    dk = GDN_HEAD_DIM
    first = t == 0

    qc_ref[...] = _conv_silu(q_ref, hq_ref, ext_ref, wq_ref, first)
    kc_ref[...] = _conv_silu(k_ref, hk_ref, ext_ref, wk_ref, first)
    vc_ref[...] = _conv_silu(v_ref, hv_ref, ext_ref, wv_ref, first)

    @pl.when(first)
    def _():
        s_ref[...] = jnp.zeros_like(s_ref)

    ba = ba_ref[...]
    beta_ref[...] = jax.nn.sigmoid(ba)
    xg = ba + dtb_ref[...]
    softplus = jnp.maximum(xg, 0.0) + jnp.log1p(jnp.exp(-jnp.abs(xg)))
    g = -jnp.exp(alog_ref[...]) * softplus
    row = lax.broadcasted_iota(jnp.int32, (tb, tb), 0)
    col = lax.broadcasted_iota(jnp.int32, (tb, tb), 1)
    shift = int(math.log2(GDN_CHUNK))
    same = jnp.right_shift(row, shift) == jnp.right_shift(col, shift)
    incl = same & (col <= row)
    strict = same & (col < row)
    tri = jnp.where(incl, 1.0, 0.0).astype(BF16)
    g1, g2, g3 = _split3(g)
    gc = _dot(tri, g1) + (_dot(tri, g2) + _dot(tri, g3))
    gc_ref[...] = gc
    gct_ref[...] = gc.T

    lane = lax.broadcasted_iota(jnp.int32, (tb, LANES), 1)
    eye = jnp.where(row == col, 1.0, 0.0)

    def head(h, s, done):
        c0 = pl.multiple_of(h * dk, dk)
        q = qc_ref[:, pl.ds(c0, dk)]
        k = kc_ref[:, pl.ds(c0, dk)]
        v = vc_ref[:, pl.ds(c0, dk)]
        z = z_ref[:, pl.ds(c0, dk)]
        q = q * lax.rsqrt(jnp.sum(q * q, axis=-1, keepdims=True) + EPS) * (dk ** -0.5)
        k = k * lax.rsqrt(jnp.sum(k * k, axis=-1, keepdims=True) + EPS)
        gcol = jnp.sum(jnp.where(lane == GDN_HEADS + h, gc_ref[...], 0.0), axis=-1, keepdims=True)
        bcol = jnp.sum(jnp.where(lane == h, beta_ref[...], 0.0), axis=-1, keepdims=True)
        grow = gct_ref[pl.ds(GDN_HEADS + h, 1), :]

        decay = jnp.where(incl, jnp.exp(jnp.where(incl, gcol - grow, 0.0)), 0.0)
        kb = k * bcol
        k16 = k.astype(BF16)
        kk = _dot_nt(kb.astype(BF16), k16)
        qk = _dot_nt(q.astype(BF16), k16)
        yield
        a_intra = (qk * decay).astype(BF16)
        m = -jnp.where(strict, kk * decay, 0.0)
        p = eye + m
        for _ in range(int(math.log2(GDN_CHUNK)) - 1):
            m16 = m.astype(BF16)
            m = _dot(m16, m16)
            yield
            p = p + _dot(p.astype(BF16), m.astype(BF16))
        eg = jnp.exp(gcol)
        rhs = jnp.concatenate([v * bcol, kb * eg], axis=1).astype(BF16)
        yield
        uw = _dot(p.astype(BF16), rhs)
        yield
        u = uw[:, :dk]
        w16 = uw[:, dk:].astype(BF16)
        qg16 = (q * eg).astype(BF16)

        zero16 = jnp.zeros((GDN_CHUNK, dk), BF16)
        outs = []
        for c in range(nchunk):
            r0, r1 = c * GDN_CHUNK, (c + 1) * GDN_CHUNK
            s16 = s.astype(BF16)
            ws = _dot(w16[r0:r1], s16)
            qs = _dot(qg16[r0:r1], s16)
            yield
            v_new = (u[r0:r1] - ws).astype(BF16)
            vn_full = jnp.concatenate([zero16] * c + [v_new] + [zero16] * (nchunk - 1 - c), axis=0)
            gl = gcol[r1 - 1:r1, :]
            kg = k[r0:r1] * jnp.exp(gl - gcol[r0:r1])
            s = s * jnp.exp(gl) + _dot_tn(kg.astype(BF16), v_new)
            outs.append(qs + _dot(a_intra[r0:r1], vn_full))
            yield
        o = jnp.concatenate(outs, axis=0)
        o = _rms(o, nw_ref[...]) * (z * jax.nn.sigmoid(z))
        done.append((o.astype(o_ref.dtype), s))

    def head_group(i, carry):
        hs = [i * HEADS_PER_TRIP + u for u in range(HEADS_PER_TRIP)]
        results = []
        live = [head(h, s_ref[h], results) for h in hs]
        while live:
            live = [g for g in live if next(g, live) is not live]
        for h, (o, s) in zip(hs, results):
            s_ref[h] = s
            o_ref[:, pl.ds(pl.multiple_of(h * dk, dk), dk)] = o
        return carry

    lax.fori_loop(0, GDN_HEADS // HEADS_PER_TRIP, head_group, 0)


def _gdn(proj, ba, conv_w, a_log, dt_bias, norm_w, bsz, seq):
    tb = TB_GDN
    nt = seq // tb
    gw = GDN_WIDTH
    alog = jnp.zeros((1, LANES), F32).at[0, GDN_HEADS:2 * GDN_HEADS].set(a_log)
    dtb = jnp.zeros((1, LANES), F32).at[0, GDN_HEADS:2 * GDN_HEADS].set(dt_bias)

    def rows(b, t):
        return b * nt + t

    return pl.pallas_call(
        _gdn_kernel,
        out_shape=jax.ShapeDtypeStruct((bsz * seq, gw), BF16),
        grid=(bsz, nt),
        in_specs=[
            pl.BlockSpec((tb, gw), lambda b, t: (rows(b, t), 0)),
            pl.BlockSpec((tb, gw), lambda b, t: (rows(b, t), 1)),
            pl.BlockSpec((tb, gw), lambda b, t: (rows(b, t), 2)),
            pl.BlockSpec((tb, gw), lambda b, t: (rows(b, t), 3)),
            pl.BlockSpec((tb, LANES), lambda b, t: (rows(b, t), 0)),
            pl.BlockSpec((GDN_CONV, gw), lambda b, t: (0, 0)),
            pl.BlockSpec((GDN_CONV, gw), lambda b, t: (0, 1)),
            pl.BlockSpec((GDN_CONV, gw), lambda b, t: (0, 2)),
            pl.BlockSpec((1, LANES), lambda b, t: (0, 0)),
            pl.BlockSpec((1, LANES), lambda b, t: (0, 0)),
            pl.BlockSpec((1, GDN_HEAD_DIM), lambda b, t: (0, 0)),
        ],
        out_specs=pl.BlockSpec((tb, gw), lambda b, t: (rows(b, t), 0)),
        scratch_shapes=[
            pltpu.VMEM((CONV_HALO, gw), F32), pltpu.VMEM((CONV_HALO, gw), F32), pltpu.VMEM((CONV_HALO, gw), F32),
            pltpu.VMEM((CONV_HALO + tb, gw), F32),
            pltpu.VMEM((tb, gw), F32), pltpu.VMEM((tb, gw), F32), pltpu.VMEM((tb, gw), F32),
            pltpu.VMEM((tb, LANES), F32), pltpu.VMEM((LANES, tb), F32), pltpu.VMEM((tb, LANES), F32),
            pltpu.VMEM((GDN_HEADS, GDN_HEAD_DIM, GDN_HEAD_DIM), F32),
        ],
        compiler_params=_cparams(("parallel", "arbitrary")),
        name="gdn",
    )(proj, proj, proj, proj, ba, conv_w, conv_w, conv_w, alog, dtb, norm_w.reshape(1, GDN_HEAD_DIM))


def _moba_prep_kernel(q_ref, k_ref, v_ref, qt_ref, k16_ref, vt_ref, km_ref):
    t = pl.program_id(1)
    qt_ref[...] = (q_ref[...] * (MOBA_HEAD_DIM ** -0.5)).T.astype(BF16)
    vt_ref[...] = v_ref[...].T.astype(BF16)
    k = k_ref[...]
    k16_ref[...] = k.astype(BF16)
    km_ref[pl.ds(t, 1), :] = jnp.mean(k, axis=0, keepdims=True)


def _moba_prep(proj, bsz, seq):
    blk = MOBA_BLOCK
    nb = seq // blk
    w = MOBA_WIDTH
    cb = COL_B // w
    return pl.pallas_call(
        _moba_prep_kernel,
        out_shape=(jax.ShapeDtypeStruct((bsz, w, seq), BF16),
                   jax.ShapeDtypeStruct((bsz * seq, w), BF16),
                   jax.ShapeDtypeStruct((bsz, w, seq), BF16),
                   jax.ShapeDtypeStruct((bsz, nb, w), F32)),
        grid=(bsz, nb),
        in_specs=[pl.BlockSpec((blk, w), lambda b, t: (b * nb + t, cb)),
                  pl.BlockSpec((blk, w), lambda b, t: (b * nb + t, cb + 1)),
                  pl.BlockSpec((blk, w), lambda b, t: (b * nb + t, cb + 2))],
        out_specs=(pl.BlockSpec((None, w, blk), lambda b, t: (b, 0, t)),
                   pl.BlockSpec((blk, w), lambda b, t: (b * nb + t, 0)),
                   pl.BlockSpec((None, w, blk), lambda b, t: (b, 0, t)),
                   pl.BlockSpec((None, nb, w), lambda b, t: (b, 0, 0))),
        compiler_params=_cparams(("parallel", "arbitrary")),
        name="moba_prep",
    )(proj, proj, proj)


def _moba_kernel(qi_ref, j_ref, qt_ref, k_ref, vt_ref, km_ref, bown_ref, bprev_ref, bfar_ref, o_ref,
                 m_ref, l_ref, acc_ref, msk_ref):
    step = pl.program_id(1)
    qi = qi_ref[step]
    j = j_ref[step]
    blk = MOBA_BLOCK
    hd = MOBA_HEAD_DIM
    nb = km_ref.shape[0]

    def rows(h):
        return slice(h * hd, (h + 1) * hd)

    def update(bias):
        heads = range(MOBA_HEADS)
        old = [(m_ref[h], l_ref[h], acc_ref[h]) for h in heads]
        scores = {}

        def score(h):
            if h < MOBA_HEADS:
                scores[h] = _dot(k_ref[:, rows(h)], qt_ref[rows(h), :])

        score(0)
        new = []
        for h in heads:
            score(h + 1)
            m_prev, l_prev, acc_prev = old[h]
            s = scores[h] + bias(h)
            m_new = jnp.maximum(m_prev, jnp.max(s, axis=0, keepdims=True))
            alpha = jnp.exp(m_prev - m_new)
            p = jnp.exp(s - m_new)
            new.append((m_new, alpha * l_prev + jnp.sum(p, axis=0, keepdims=True),
                        alpha * acc_prev + _dot(vt_ref[rows(h), :], p.astype(BF16))))
        for h in heads:
            m_ref[h], l_ref[h], acc_ref[h] = new[h]

    @pl.when(j == 0)
    def _():
        blkid = lax.broadcasted_iota(jnp.int32, (nb, blk), 0)
        for h in range(MOBA_HEADS):
            m_ref[h] = jnp.full((1, blk), NEG_INF, F32)
            l_ref[h] = jnp.zeros((1, blk), F32)
            acc_ref[h] = jnp.zeros((hd, blk), F32)
            k1, k2, _ = _split3(km_ref[:, rows(h)])
            qt = qt_ref[rows(h), :]
            gate = _dot(k1, qt) + _dot(k2, qt)
            gate = jnp.where(blkid < qi, gate, -jnp.inf)
            cnt = jnp.zeros((nb, blk), F32)
            for i in range(nb):
                gi = gate[i:i + 1, :]
                ahead = (gi > gate) | ((gi == gate) & (blkid > i))
                cnt = cnt + jnp.where(ahead, 1.0, 0.0)
            msk_ref[h] = jnp.where(cnt < MOBA_TOPK, 0.0, NEG_INF)
        update(lambda h: bown_ref[h])

    @pl.when(j == 1)
    def _():
        update(lambda h: bprev_ref[h] + msk_ref[h, pl.ds(qi - j, 1), :])

    @pl.when(j >= 2)
    def _():
        update(lambda h: bfar_ref[h] + msk_ref[h, pl.ds(qi - j, 1), :])

    @pl.when(j == qi)
    def _():
        for h in range(MOBA_HEADS):
            o_ref[:, rows(h)] = (acc_ref[h] / l_ref[h]).T.astype(o_ref.dtype)


def _rel_bucket(dist):
    n = jnp.maximum(dist, 0)
    max_exact = REL_BUCKETS // 2
    nf = jnp.maximum(n, max_exact).astype(F32)
    large = max_exact + (jnp.log(nf / max_exact) / math.log(REL_MAX_DIST / max_exact)
                         * (REL_BUCKETS - max_exact)).astype(jnp.int32)
    large = jnp.minimum(large, REL_BUCKETS - 1)
    return jnp.where(n < max_exact, n, large)


def _bias_lookup(rel_bias, dist):
    bucket = _rel_bucket(dist)[None]
    out = jnp.zeros((rel_bias.shape[1],) + dist.shape, F32)
    for b in range(REL_BUCKETS):
        out = jnp.where(bucket == b, rel_bias[b].astype(F32).reshape((-1,) + (1,) * dist.ndim), out)
    return out


def _moba(qt, k16, vt, kmean, rel_bias, bsz, seq):
    blk = MOBA_BLOCK
    nb = seq // blk
    w = MOBA_WIDTH
    assert blk >= REL_MAX_DIST
    kk = jnp.arange(blk)[:, None]
    qq = jnp.arange(blk)[None, :]
    b_own = jnp.where((qq >= kk)[None], _bias_lookup(rel_bias, qq - kk), NEG_INF)
    b_prev = _bias_lookup(rel_bias, blk + qq - kk)
    b_far = jnp.broadcast_to(_bias_lookup(rel_bias, jnp.full((1, 1), 2 * blk)), (MOBA_HEADS, 1, blk))
    qi_list, j_list = [], []
    for qi in range(nb):
        for j in range(qi + 1):
            qi_list.append(qi)
            j_list.append(j)
    qi_arr = jnp.asarray(np.array(qi_list, np.int32))
    j_arr = jnp.asarray(np.array(j_list, np.int32))

    grid_spec = pltpu.PrefetchScalarGridSpec(
        num_scalar_prefetch=2,
        grid=(bsz, len(qi_list)),
        in_specs=[
            pl.BlockSpec((None, w, blk), lambda b, s, qi, j: (b, 0, qi[s])),
            pl.BlockSpec((blk, w), lambda b, s, qi, j: (b * nb + qi[s] - j[s], 0)),
            pl.BlockSpec((None, w, blk), lambda b, s, qi, j: (b, 0, qi[s] - j[s])),
            pl.BlockSpec((None, nb, w), lambda b, s, qi, j: (b, 0, 0)),
            pl.BlockSpec((MOBA_HEADS, blk, blk), lambda b, s, qi, j: (0, 0, 0)),
            pl.BlockSpec((MOBA_HEADS, blk, blk), lambda b, s, qi, j: (0, 0, 0)),
            pl.BlockSpec((MOBA_HEADS, 1, blk), lambda b, s, qi, j: (0, 0, 0)),
        ],
        out_specs=pl.BlockSpec((blk, w), lambda b, s, qi, j: (b * nb + qi[s], 0)),
        scratch_shapes=[
            pltpu.VMEM((MOBA_HEADS, 1, blk), F32),
            pltpu.VMEM((MOBA_HEADS, 1, blk), F32),
            pltpu.VMEM((MOBA_HEADS, MOBA_HEAD_DIM, blk), F32),
            pltpu.VMEM((MOBA_HEADS, nb, blk), F32),
        ],
    )
    return pl.pallas_call(
        _moba_kernel,
        out_shape=jax.ShapeDtypeStruct((bsz * seq, w), BF16),
        grid_spec=grid_spec,
        compiler_params=_cparams(("parallel", "arbitrary")),
        name="moba",
    )(qi_arr, j_arr, qt, k16, vt, kmean, b_own, b_prev, b_far)


def _sconv_kernel(x_ref, bg_ref, cg_ref, w_ref, o_ref, ext_ref):
    seq = x_ref.shape[0]
    u = cg_ref[...] * x_ref[...]
    ext_ref[0:CONV_HALO, :] = jnp.zeros((CONV_HALO, u.shape[1]), F32)
    ext_ref[CONV_HALO:, :] = u
    width = w_ref.shape[0]
    y = u * w_ref[width - 1:width, :]
    for s in range(1, width):
        y = y + ext_ref[CONV_HALO - s:CONV_HALO - s + seq, :] * w_ref[width - 1 - s:width - s, :]
    o_ref[...] = (bg_ref[...] * y).astype(o_ref.dtype)


def _sconv(proj, conv_w, bsz, seq):
    cw = SCONV_WIDTH
    nc = cw // LANES
    c0 = COL_C // LANES
    return pl.pallas_call(
        _sconv_kernel,
        out_shape=jax.ShapeDtypeStruct((bsz * seq, cw), BF16),
        grid=(bsz, nc),
        in_specs=[
            pl.BlockSpec((seq, LANES), lambda b, c: (b, c0 + c)),
            pl.BlockSpec((seq, LANES), lambda b, c: (b, c0 + nc + c)),
            pl.BlockSpec((seq, LANES), lambda b, c: (b, c0 + 2 * nc + c)),
            pl.BlockSpec((SCONV_K, LANES), lambda b, c: (0, c)),
        ],
        out_specs=pl.BlockSpec((seq, LANES), lambda b, c: (b, c)),
        scratch_shapes=[pltpu.VMEM((CONV_HALO + seq, LANES), F32)],
        compiler_params=_cparams(("parallel", "parallel")),
        name="sconv",
    )(proj, proj, proj, conv_w)


def _out_kernel(x_ref, oa_ref, ob_ref, oc_ref, w_ref, o_ref):
    a0, a1, a2 = GDN_WIDTH, GDN_WIDTH + MOBA_WIDTH, D_MODEL
    y = _dot(oa_ref[...], w_ref[0:a0, :])
    y = y + _dot(ob_ref[...], w_ref[a0:a1, :])
    y = y + _dot(oc_ref[...], w_ref[a1:a2, :])
    o_ref[...] = x_ref[...] + y


def _out(x, l, oa, ob, oc, w):
    n, d = x.shape
    return pl.pallas_call(
        _out_kernel,
        out_shape=jax.ShapeDtypeStruct((n, d), F32),
        grid=(n // TM_OUT,),
        in_specs=[
            pl.BlockSpec((TM_OUT, d), lambda i: (i, 0)),
            pl.BlockSpec((TM_OUT, GDN_WIDTH), lambda i: (i, 0)),
            pl.BlockSpec((TM_OUT, MOBA_WIDTH), lambda i: (i, 0)),
            pl.BlockSpec((TM_OUT, SCONV_WIDTH), lambda i: (i, 0)),
            pl.BlockSpec((None, d, d), lambda i: (l, 0, 0)),
        ],
        out_specs=pl.BlockSpec((TM_OUT, d), lambda i: (i, 0)),
        compiler_params=_cparams(("parallel",)),
        name="out_proj",
    )(x, oa, ob, oc, w)


def _mixer(x, l, mix_norm, w_main, w_ba, conv_qkv_w, a_log, dt_bias, gdn_norm_w, conv_c_w, w_out, rel_bias,
           bsz, seq):
    proj, ba = _proj(x, l, mix_norm, w_main, w_ba)
    o_a = _gdn(proj, ba, conv_qkv_w, a_log, dt_bias, gdn_norm_w, bsz, seq)
    o_b = _moba(*_moba_prep(proj, bsz, seq), rel_bias, bsz, seq)
    o_c = _sconv(proj, conv_c_w, bsz, seq)
    return _out(x, l, o_a, o_b, o_c, w_out)


def kernel(x, ffn1_norm, ffn1_w_gate, ffn1_w_up, ffn1_w_down, mix_norm, w_in, conv_qkv_w, a_log, dt_bias,
           gdn_norm_w, conv_c_w, w_out, ffn2_norm, ffn2_w_gate, ffn2_w_up, ffn2_w_down, rel_bias, final_norm):
    bsz, seq, d = x.shape
    x = x.reshape(bsz * seq, d)
    ffn1 = [w.astype(BF16) for w in (ffn1_w_gate, ffn1_w_up, ffn1_w_down)]
    ffn2 = [w.astype(BF16) for w in (ffn2_w_gate, ffn2_w_up, ffn2_w_down)]
    a_end = 4 * GDN_WIDTH
    ba_end = a_end + 2 * GDN_HEADS
    w_main = jnp.concatenate([w_in[..., :a_end], w_in[..., ba_end:]], axis=-1).astype(BF16)
    w_ba = jnp.pad(w_in[..., a_end:ba_end], ((0, 0), (0, 0), (0, LANES - 2 * GDN_HEADS)))
    w_ba1 = w_ba.astype(BF16)
    w_ba = jnp.stack([w_ba1, (w_ba - w_ba1.astype(F32)).astype(BF16)], axis=1)
    w_out16 = w_out.astype(BF16)
    for l in range(DEPTH):
        x = _ffn(x, l, ffn1_norm[l], *ffn1, final_norm, False)
        x = _mixer(x, l, mix_norm[l], w_main, w_ba, conv_qkv_w[l], a_log[l], dt_bias[l], gdn_norm_w[l],
                   conv_c_w[l], w_out16, rel_bias, bsz, seq)
        x = _ffn(x, l, ffn2_norm[l], *ffn2, final_norm, l == DEPTH - 1)
    return x.reshape(bsz, seq, d)
```

```python
import functools
import math

import jax
import jax.numpy as jnp
import numpy as np
from jax import lax
from jax.experimental import pallas as pl
from jax.experimental.pallas import tpu as pltpu

F32 = jnp.float32
BF16 = jnp.bfloat16

D_MODEL = 2048
DEPTH = 2
EPS = 1e-6
D_FF = 5632

GDN_HEAD_DIM = 128
GDN_HEADS = 8
GDN_WIDTH = GDN_HEADS * GDN_HEAD_DIM
GDN_CONV = 4
GDN_CHUNK = 64

MOBA_HEAD_DIM = 128
MOBA_HEADS = 4
MOBA_WIDTH = MOBA_HEADS * MOBA_HEAD_DIM
MOBA_BLOCK = 256
MOBA_TOPK = 3

SCONV_WIDTH = D_MODEL - GDN_WIDTH - MOBA_WIDTH
SCONV_K = 3

REL_BUCKETS = 32
REL_MAX_DIST = 128
NEG_INF = -1e30

LANES = 128
CONV_HALO = 8

PROJ_MAIN = 4 * GDN_WIDTH + 3 * MOBA_WIDTH + 3 * SCONV_WIDTH
COL_B = 4 * GDN_WIDTH
COL_C = COL_B + 3 * MOBA_WIDTH

VMEM_LIMIT = 56 * 1024 * 1024

TM_FFN = 512
TF_FFN = 512
TM_PROJ = 512
TN_PROJ = 1024
TM_OUT = 512
TB_GDN = 256
HEADS_PER_TRIP = 8
MOBA_KV_PER_STEP = 4
MASK_PAD_ROWS = 8


def _cparams(sem):
    return pltpu.CompilerParams(dimension_semantics=sem, vmem_limit_bytes=VMEM_LIMIT)


def _rms(x, w):
    return x * lax.rsqrt(jnp.mean(x * x, axis=-1, keepdims=True) + EPS) * w


def _split3(x):
    x1 = x.astype(BF16)
    r1 = x - x1.astype(F32)
    x2 = r1.astype(BF16)
    x3 = (r1 - x2.astype(F32)).astype(BF16)
    return x1, x2, x3


def _dot(a, b):
    return jnp.dot(a, b, preferred_element_type=F32)


def _dot_nt(a, b):
    return lax.dot_general(a, b, (((1,), (1,)), ((), ())), preferred_element_type=F32)


def _dot_tn(a, b):
    return lax.dot_general(a, b, (((0,), (0,)), ((), ())), preferred_element_type=F32)


def _ffn_kernel(x_ref, nw_ref, wg_ref, wu_ref, wd_ref, fw_ref, o_ref, hn_ref, acc_ref, *, n_f, final_norm):
    f = pl.program_id(1)

    @pl.when(f == 0)
    def _():
        hn_ref[...] = _rms(x_ref[...], nw_ref[...]).astype(BF16)
        acc_ref[...] = jnp.zeros_like(acc_ref)

    hn = hn_ref[...]
    g = _dot(hn, wg_ref[...])
    u = _dot(hn, wu_ref[...])
    a = (g * jax.nn.sigmoid(g) * u).astype(BF16)
    acc_ref[...] += _dot(a, wd_ref[...])

    @pl.when(f == n_f - 1)
    def _():
        y = x_ref[...] + 0.5 * acc_ref[...]
        o_ref[...] = _rms(y, fw_ref[...]) if final_norm else y


def _ffn(x, l, nw, wg, wu, wd, fw, final_norm):
    n, d = x.shape
    ff = wg.shape[2]
    n_f = ff // TF_FFN
    return pl.pallas_call(
        functools.partial(_ffn_kernel, n_f=n_f, final_norm=final_norm),
        out_shape=jax.ShapeDtypeStruct((n, d), F32),
        grid=(n // TM_FFN, n_f),
        in_specs=[
            pl.BlockSpec((TM_FFN, d), lambda i, f: (i, 0)),
            pl.BlockSpec((1, d), lambda i, f: (0, 0)),
            pl.BlockSpec((None, d, TF_FFN), lambda i, f: (l, 0, f)),
            pl.BlockSpec((None, d, TF_FFN), lambda i, f: (l, 0, f)),
            pl.BlockSpec((None, TF_FFN, d), lambda i, f: (l, f, 0)),
            pl.BlockSpec((1, d), lambda i, f: (0, 0)),
        ],
        out_specs=pl.BlockSpec((TM_FFN, d), lambda i, f: (i, 0)),
        scratch_shapes=[pltpu.VMEM((TM_FFN, d), BF16), pltpu.VMEM((TM_FFN, d), F32)],
        compiler_params=_cparams(("parallel", "arbitrary")),
        name="ffn",
    )(x, nw.reshape(1, d), wg, wu, wd, fw.reshape(1, d))


def _proj_kernel(x_ref, nw_ref, wa_ref, wbc_ref, wba_ref, o_ref, ba_ref, hn_ref, *, n_a):
    j = pl.program_id(1)

    @pl.when(j == 0)
    def _():
        h = _rms(x_ref[...], nw_ref[...])
        h1, h2, _ = _split3(h)
        hn_ref[...] = h1
        w = wba_ref[...]
        w1 = w.astype(BF16)
        w2 = (w - w1.astype(F32)).astype(BF16)
        ba_ref[...] = _dot(h1, w1) + (_dot(h1, w2) + _dot(h2, w1))

    @pl.when(j < n_a)
    def _():
        o_ref[...] = _dot(hn_ref[...], wa_ref[...])

    @pl.when(j >= n_a)
    def _():
        o_ref[...] = _dot(hn_ref[...], wbc_ref[...])


def _proj(x, l, nw, w_in, w_in16, w_bc16):
    n, d = x.shape
    n_a = COL_B // TN_PROJ
    ba_blk = COL_B // LANES
    return pl.pallas_call(
        functools.partial(_proj_kernel, n_a=n_a),
        out_shape=(jax.ShapeDtypeStruct((n, PROJ_MAIN), F32), jax.ShapeDtypeStruct((n, LANES), F32)),
        grid=(n // TM_PROJ, PROJ_MAIN // TN_PROJ),
        in_specs=[
            pl.BlockSpec((TM_PROJ, d), lambda i, j: (i, 0)),
            pl.BlockSpec((1, d), lambda i, j: (0, 0)),
            pl.BlockSpec((None, d, TN_PROJ), lambda i, j: (l, 0, jnp.minimum(j, n_a - 1))),
            pl.BlockSpec((None, d, TN_PROJ), lambda i, j: (l, 0, jnp.maximum(j - n_a, 0))),
            pl.BlockSpec((None, d, LANES), lambda i, j: (l, 0, ba_blk)),
        ],
        out_specs=(pl.BlockSpec((TM_PROJ, TN_PROJ), lambda i, j: (i, j)),
                   pl.BlockSpec((TM_PROJ, LANES), lambda i, j: (i, 0))),
        scratch_shapes=[pltpu.VMEM((TM_PROJ, d), BF16)],
        compiler_params=_cparams(("parallel", "arbitrary")),
        name="proj",
    )(x, nw.reshape(1, d), w_in16, w_bc16, w_in)


def _conv_silu(x_ref, hist_ref, ext_ref, w_ref, first):
    tb = x_ref.shape[0]

    @pl.when(first)
    def _():
        hist_ref[...] = jnp.zeros_like(hist_ref)

    x = x_ref[...]
    ext_ref[0:CONV_HALO, :] = hist_ref[...]
    ext_ref[CONV_HALO:, :] = x
    hist_ref[...] = x[tb - CONV_HALO:, :]
    width = w_ref.shape[0]
    y = x * w_ref[width - 1:width, :]
    for s in range(1, width):
        y = y + ext_ref[CONV_HALO - s:CONV_HALO - s + tb, :] * w_ref[width - 1 - s:width - s, :]
    return y * jax.nn.sigmoid(y)


def _gdn_kernel(q_ref, k_ref, v_ref, z_ref, ba_ref, wq_ref, wk_ref, wv_ref, alog_ref, dtb_ref, nw_ref,
                o_ref,
                hq_ref, hk_ref, hv_ref, ext_ref, qc_ref, kc_ref, vc_ref, gc_ref, gct_ref, beta_ref,
                s_ref):
    t = pl.program_id(1)
    tb = q_ref.shape[0]
    nchunk = tb // GDN_CHUNK
    dk = GDN_HEAD_DIM
    first = t == 0

    qc_ref[...] = _conv_silu(q_ref, hq_ref, ext_ref, wq_ref, first)
    kc_ref[...] = _conv_silu(k_ref, hk_ref, ext_ref, wk_ref, first)
    vc_ref[...] = _conv_silu(v_ref, hv_ref, ext_ref, wv_ref, first)

    @pl.when(first)
    def _():
        s_ref[...] = jnp.zeros_like(s_ref)

    ba = ba_ref[...]
    beta_ref[...] = jax.nn.sigmoid(ba)
    xg = ba + dtb_ref[...]
    softplus = jnp.maximum(xg, 0.0) + jnp.log1p(jnp.exp(-jnp.abs(xg)))
    g = -jnp.exp(alog_ref[...]) * softplus
    row = lax.broadcasted_iota(jnp.int32, (tb, tb), 0)
    col = lax.broadcasted_iota(jnp.int32, (tb, tb), 1)
    shift = int(math.log2(GDN_CHUNK))
    same = jnp.right_shift(row, shift) == jnp.right_shift(col, shift)
    incl = same & (col <= row)
    strict = same & (col < row)
    tri = jnp.where(incl, 1.0, 0.0).astype(BF16)
    g1, g2, g3 = _split3(g)
    gc = _dot(tri, g1) + (_dot(tri, g2) + _dot(tri, g3))
    gc_ref[...] = gc
    gct_ref[...] = gc.T

    lane = lax.broadcasted_iota(jnp.int32, (tb, LANES), 1)
    eye = jnp.where(row == col, 1.0, 0.0)

    def head(h, s, done):
        c0 = pl.multiple_of(h * dk, dk)
        q = qc_ref[:, pl.ds(c0, dk)]
        k = kc_ref[:, pl.ds(c0, dk)]
        v = vc_ref[:, pl.ds(c0, dk)]
        z = z_ref[:, pl.ds(c0, dk)]
        q = q * lax.rsqrt(jnp.sum(q * q, axis=-1, keepdims=True) + EPS) * (dk ** -0.5)
        k = k * lax.rsqrt(jnp.sum(k * k, axis=-1, keepdims=True) + EPS)
        gcol = jnp.sum(jnp.where(lane == GDN_HEADS + h, gc_ref[...], 0.0), axis=-1, keepdims=True)
        bcol = jnp.sum(jnp.where(lane == h, beta_ref[...], 0.0), axis=-1, keepdims=True)
        grow = gct_ref[pl.ds(GDN_HEADS + h, 1), :]

        decay = jnp.where(incl, jnp.exp(jnp.where(incl, gcol - grow, 0.0)), 0.0)
        kb = k * bcol
        k16 = k.astype(BF16)
        kk = _dot_nt(kb.astype(BF16), k16)
        qk = _dot_nt(q.astype(BF16), k16)
        yield
        a_intra = (qk * decay).astype(BF16)
        m = -jnp.where(strict, kk * decay, 0.0)
        p = eye + m
        for _ in range(int(math.log2(GDN_CHUNK)) - 1):
            m16 = m.astype(BF16)
            m = _dot(m16, m16)
            yield
            p = p + _dot(p.astype(BF16), m.astype(BF16))
        eg = jnp.exp(gcol)
        rhs = jnp.concatenate([v * bcol, kb * eg], axis=1).astype(BF16)
        yield
        uw = _dot(p.astype(BF16), rhs)
        yield
        u = uw[:, :dk]
        w16 = uw[:, dk:].astype(BF16)
        qg16 = (q * eg).astype(BF16)

        zero16 = jnp.zeros((GDN_CHUNK, dk), BF16)
        outs = []
        for c in range(nchunk):
            r0, r1 = c * GDN_CHUNK, (c + 1) * GDN_CHUNK
            s16 = s.astype(BF16)
            ws = _dot(w16[r0:r1], s16)
            qs = _dot(qg16[r0:r1], s16)
            yield
            v_new = (u[r0:r1] - ws).astype(BF16)
            vn_full = jnp.concatenate([zero16] * c + [v_new] + [zero16] * (nchunk - 1 - c), axis=0)
            gl = gcol[r1 - 1:r1, :]
            kg = k[r0:r1] * jnp.exp(gl - gcol[r0:r1])
            s = s * jnp.exp(gl) + _dot_tn(kg.astype(BF16), v_new)
            outs.append(qs + _dot(a_intra[r0:r1], vn_full))
            yield
        o = jnp.concatenate(outs, axis=0)
        o = _rms(o, nw_ref[...]) * (z * jax.nn.sigmoid(z))
        done.append((o.astype(o_ref.dtype), s))

    def head_group(i, carry):
        hs = [i * HEADS_PER_TRIP + u for u in range(HEADS_PER_TRIP)]
        results = []
        live = [head(h, s_ref[h], results) for h in hs]
        while live:
            live = [g for g in live if next(g, live) is not live]
        for h, (o, s) in zip(hs, results):
            s_ref[h] = s
            o_ref[:, pl.ds(pl.multiple_of(h * dk, dk), dk)] = o
        return carry

    lax.fori_loop(0, GDN_HEADS // HEADS_PER_TRIP, head_group, 0)


def _gdn(proj, ba, conv_w, a_log, dt_bias, norm_w, bsz, seq):
    tb = TB_GDN
    nt = seq // tb
    gw = GDN_WIDTH
    alog = jnp.zeros((1, LANES), F32).at[0, GDN_HEADS:2 * GDN_HEADS].set(a_log)
    dtb = jnp.zeros((1, LANES), F32).at[0, GDN_HEADS:2 * GDN_HEADS].set(dt_bias)

    def rows(b, t):
        return b * nt + t

    return pl.pallas_call(
        _gdn_kernel,
        out_shape=jax.ShapeDtypeStruct((bsz * seq, gw), BF16),
        grid=(bsz, nt),
        in_specs=[
            pl.BlockSpec((tb, gw), lambda b, t: (rows(b, t), 0)),
            pl.BlockSpec((tb, gw), lambda b, t: (rows(b, t), 1)),
            pl.BlockSpec((tb, gw), lambda b, t: (rows(b, t), 2)),
            pl.BlockSpec((tb, gw), lambda b, t: (rows(b, t), 3)),
            pl.BlockSpec((tb, LANES), lambda b, t: (rows(b, t), 0)),
            pl.BlockSpec((GDN_CONV, gw), lambda b, t: (0, 0)),
            pl.BlockSpec((GDN_CONV, gw), lambda b, t: (0, 1)),
            pl.BlockSpec((GDN_CONV, gw), lambda b, t: (0, 2)),
            pl.BlockSpec((1, LANES), lambda b, t: (0, 0)),
            pl.BlockSpec((1, LANES), lambda b, t: (0, 0)),
            pl.BlockSpec((1, GDN_HEAD_DIM), lambda b, t: (0, 0)),
        ],
        out_specs=pl.BlockSpec((tb, gw), lambda b, t: (rows(b, t), 0)),
        scratch_shapes=[
            pltpu.VMEM((CONV_HALO, gw), F32), pltpu.VMEM((CONV_HALO, gw), F32), pltpu.VMEM((CONV_HALO, gw), F32),
            pltpu.VMEM((CONV_HALO + tb, gw), F32),
            pltpu.VMEM((tb, gw), F32), pltpu.VMEM((tb, gw), F32), pltpu.VMEM((tb, gw), F32),
            pltpu.VMEM((tb, LANES), F32), pltpu.VMEM((LANES, tb), F32), pltpu.VMEM((tb, LANES), F32),
            pltpu.VMEM((GDN_HEADS, GDN_HEAD_DIM, GDN_HEAD_DIM), F32),
        ],
        compiler_params=_cparams(("parallel", "arbitrary")),
        name="gdn",
    )(proj, proj, proj, proj, ba, conv_w, conv_w, conv_w, alog, dtb, norm_w.reshape(1, GDN_HEAD_DIM))


def _moba_prep_kernel(q_ref, k_ref, v_ref, qt_ref, k16_ref, vt_ref, km_ref):
    t = pl.program_id(1)
    qt_ref[...] = (q_ref[...] * (MOBA_HEAD_DIM ** -0.5)).T.astype(BF16)
    vt_ref[...] = v_ref[...].T.astype(BF16)
    k = k_ref[...]
    k16_ref[...] = k.astype(BF16)
    km_ref[pl.ds(t, 1), :] = jnp.mean(k, axis=0, keepdims=True)


def _moba_prep(proj, bsz, seq):
    blk = MOBA_BLOCK
    nb = seq // blk
    w = MOBA_WIDTH
    cb = COL_B // w
    return pl.pallas_call(
        _moba_prep_kernel,
        out_shape=(jax.ShapeDtypeStruct((bsz, w, seq), BF16),
                   jax.ShapeDtypeStruct((bsz * seq, w), BF16),
                   jax.ShapeDtypeStruct((bsz, w, seq), BF16),
                   jax.ShapeDtypeStruct((bsz, nb, w), F32)),
        grid=(bsz, nb),
        in_specs=[pl.BlockSpec((blk, w), lambda b, t: (b * nb + t, cb)),
                  pl.BlockSpec((blk, w), lambda b, t: (b * nb + t, cb + 1)),
                  pl.BlockSpec((blk, w), lambda b, t: (b * nb + t, cb + 2))],
        out_specs=(pl.BlockSpec((None, w, blk), lambda b, t: (b, 0, t)),
                   pl.BlockSpec((blk, w), lambda b, t: (b * nb + t, 0)),
                   pl.BlockSpec((None, w, blk), lambda b, t: (b, 0, t)),
                   pl.BlockSpec((None, nb, w), lambda b, t: (b, 0, 0))),
        compiler_params=_cparams(("parallel", "arbitrary")),
        name="moba_prep",
    )(proj, proj, proj)


def _moba_kernel(qi_ref, j0_ref, *refs):
    nkv = MOBA_KV_PER_STEP
    qt_ref = refs[0]
    k_refs = refs[1:1 + nkv]
    vt_refs = refs[1 + nkv:1 + 2 * nkv]
    km_ref, bown_ref, bprev_ref, bfar_ref, o_ref, m_ref, l_ref, acc_ref, msk_ref = refs[1 + 2 * nkv:]
    step = pl.program_id(1)
    qi = qi_ref[step]
    j0 = j0_ref[step]
    blk = MOBA_BLOCK
    hd = MOBA_HEAD_DIM
    nb = km_ref.shape[0]
    heads = range(MOBA_HEADS)

    def rows(h):
        return slice(h * hd, (h + 1) * hd)

    def mask_row(h, j):
        return msk_ref[h, pl.ds(jnp.where(j <= qi, qi - j, nb), 1), :]

    def update(bias):
        old = [(m_ref[h], l_ref[h], acc_ref[h]) for h in heads]
        new = []
        for h in heads:
            m_prev, l_prev, acc_prev = old[h]
            qt = qt_ref[rows(h), :]
            s = jnp.concatenate([_dot(k_refs[u][:, rows(h)], qt) + bias(h, u) for u in range(nkv)], axis=0)
            m_new = jnp.maximum(m_prev, jnp.max(s, axis=0, keepdims=True))
            alpha = jnp.exp(m_prev - m_new)
            p = jnp.exp(s - m_new)
            vt = jnp.concatenate([vt_refs[u][rows(h), :] for u in range(nkv)], axis=1)
            new.append((m_new, alpha * l_prev + jnp.sum(p, axis=0, keepdims=True),
                        alpha * acc_prev + _dot(vt, p.astype(BF16))))
        for h in heads:
            m_ref[h], l_ref[h], acc_ref[h] = new[h]

    @pl.when(j0 == 0)
    def _():
        blkid = lax.broadcasted_iota(jnp.int32, (nb, blk), 0)
        for h in heads:
            m_ref[h] = jnp.full((1, blk), NEG_INF, F32)
            l_ref[h] = jnp.zeros((1, blk), F32)
            acc_ref[h] = jnp.zeros((hd, blk), F32)
            k1, k2, _ = _split3(km_ref[:, rows(h)])
            qt = qt_ref[rows(h), :]
            gate = _dot(k1, qt) + _dot(k2, qt)
            gate = jnp.where(blkid < qi, gate, -jnp.inf)
            cnt = jnp.zeros((nb, blk), F32)
            for i in range(nb):
                gi = gate[i:i + 1, :]
                ahead = (gi > gate) | ((gi == gate) & (blkid > i))
                cnt = cnt + jnp.where(ahead, 1.0, 0.0)
            msk_ref[h, 0:nb, :] = jnp.where(cnt < MOBA_TOPK, 0.0, NEG_INF)
            msk_ref[h, nb:, :] = jnp.full((msk_ref.shape[1] - nb, blk), NEG_INF, F32)

        def bias(h, u):
            if u == 0:
                return bown_ref[h]
            return (bprev_ref[h] if u == 1 else bfar_ref[h]) + mask_row(h, u)

        update(bias)

    @pl.when(j0 > 0)
    def _():
        update(lambda h, u: bfar_ref[h] + mask_row(h, j0 + u))

    @pl.when(j0 + nkv > qi)
    def _():
        for h in heads:
            o_ref[:, rows(h)] = (acc_ref[h] / l_ref[h]).T.astype(o_ref.dtype)


def _rel_bucket(dist):
    n = jnp.maximum(dist, 0)
    max_exact = REL_BUCKETS // 2
    nf = jnp.maximum(n, max_exact).astype(F32)
    large = max_exact + (jnp.log(nf / max_exact) / math.log(REL_MAX_DIST / max_exact)
                         * (REL_BUCKETS - max_exact)).astype(jnp.int32)
    large = jnp.minimum(large, REL_BUCKETS - 1)
    return jnp.where(n < max_exact, n, large)


def _bias_lookup(rel_bias, dist):
    bucket = _rel_bucket(dist)[None]
    out = jnp.zeros((rel_bias.shape[1],) + dist.shape, F32)
    for b in range(REL_BUCKETS):
        out = jnp.where(bucket == b, rel_bias[b].astype(F32).reshape((-1,) + (1,) * dist.ndim), out)
    return out


def _moba(qt, k16, vt, kmean, rel_bias, bsz, seq):
    blk = MOBA_BLOCK
    nb = seq // blk
    w = MOBA_WIDTH
    assert blk >= REL_MAX_DIST
    kk = jnp.arange(blk)[:, None]
    qq = jnp.arange(blk)[None, :]
    b_own = jnp.where((qq >= kk)[None], _bias_lookup(rel_bias, qq - kk), NEG_INF)
    b_prev = _bias_lookup(rel_bias, blk + qq - kk)
    b_far = jnp.broadcast_to(_bias_lookup(rel_bias, jnp.full((1, 1), 2 * blk)), (MOBA_HEADS, 1, blk))
    nkv = MOBA_KV_PER_STEP
    steps = [(qi, j0) for qi in range(nb) for j0 in range(0, qi + 1, nkv)]
    qi_arr = jnp.asarray(np.array([s[0] for s in steps], np.int32))
    j0_arr = jnp.asarray(np.array([s[1] for s in steps], np.int32))

    def kblk(u):
        return lambda b, s, qi, j0: (b * nb + jnp.maximum(qi[s] - j0[s] - u, 0), 0)

    def vblk(u):
        return lambda b, s, qi, j0: (b, 0, jnp.maximum(qi[s] - j0[s] - u, 0))

    grid_spec = pltpu.PrefetchScalarGridSpec(
        num_scalar_prefetch=2,
        grid=(bsz, len(steps)),
        in_specs=(
            [pl.BlockSpec((None, w, blk), lambda b, s, qi, j0: (b, 0, qi[s]))]
            + [pl.BlockSpec((blk, w), kblk(u)) for u in range(nkv)]
            + [pl.BlockSpec((None, w, blk), vblk(u)) for u in range(nkv)]
            + [pl.BlockSpec((None, nb, w), lambda b, s, qi, j0: (b, 0, 0)),
               pl.BlockSpec((MOBA_HEADS, blk, blk), lambda b, s, qi, j0: (0, 0, 0)),
               pl.BlockSpec((MOBA_HEADS, blk, blk), lambda b, s, qi, j0: (0, 0, 0)),
               pl.BlockSpec((MOBA_HEADS, 1, blk), lambda b, s, qi, j0: (0, 0, 0))]),
        out_specs=pl.BlockSpec((blk, w), lambda b, s, qi, j0: (b * nb + qi[s], 0)),
        scratch_shapes=[
            pltpu.VMEM((MOBA_HEADS, 1, blk), F32),
            pltpu.VMEM((MOBA_HEADS, 1, blk), F32),
            pltpu.VMEM((MOBA_HEADS, MOBA_HEAD_DIM, blk), F32),
            pltpu.VMEM((MOBA_HEADS, nb + MASK_PAD_ROWS, blk), F32),
        ],
    )
    return pl.pallas_call(
        _moba_kernel,
        out_shape=jax.ShapeDtypeStruct((bsz * seq, w), BF16),
        grid_spec=grid_spec,
        compiler_params=_cparams(("parallel", "arbitrary")),
        name="moba",
    )(qi_arr, j0_arr, qt, *([k16] * nkv), *([vt] * nkv), kmean, b_own, b_prev, b_far)


def _sconv_kernel(x_ref, bg_ref, cg_ref, w_ref, o_ref, ext_ref):
    seq = x_ref.shape[0]
    u = cg_ref[...] * x_ref[...]
    ext_ref[0:CONV_HALO, :] = jnp.zeros((CONV_HALO, u.shape[1]), F32)
    ext_ref[CONV_HALO:, :] = u
    width = w_ref.shape[0]
    y = u * w_ref[width - 1:width, :]
    for s in range(1, width):
        y = y + ext_ref[CONV_HALO - s:CONV_HALO - s + seq, :] * w_ref[width - 1 - s:width - s, :]
    o_ref[...] = (bg_ref[...] * y).astype(o_ref.dtype)


def _sconv(proj, conv_w, bsz, seq):
    cw = SCONV_WIDTH
    nc = cw // LANES
    c0 = COL_C // LANES
    return pl.pallas_call(
        _sconv_kernel,
        out_shape=jax.ShapeDtypeStruct((bsz * seq, cw), BF16),
        grid=(bsz, nc),
        in_specs=[
            pl.BlockSpec((seq, LANES), lambda b, c: (b, c0 + c)),
            pl.BlockSpec((seq, LANES), lambda b, c: (b, c0 + nc + c)),
            pl.BlockSpec((seq, LANES), lambda b, c: (b, c0 + 2 * nc + c)),
            pl.BlockSpec((SCONV_K, LANES), lambda b, c: (0, c)),
        ],
        out_specs=pl.BlockSpec((seq, LANES), lambda b, c: (b, c)),
        scratch_shapes=[pltpu.VMEM((CONV_HALO + seq, LANES), F32)],
        compiler_params=_cparams(("parallel", "parallel")),
        name="sconv",
    )(proj, proj, proj, conv_w)


def _out_kernel(x_ref, oa_ref, ob_ref, oc_ref, w_ref, o_ref):
    a0, a1, a2 = GDN_WIDTH, GDN_WIDTH + MOBA_WIDTH, D_MODEL
    y = _dot(oa_ref[...], w_ref[0:a0, :])
    y = y + _dot(ob_ref[...], w_ref[a0:a1, :])
    y = y + _dot(oc_ref[...], w_ref[a1:a2, :])
    o_ref[...] = x_ref[...] + y


def _out(x, l, oa, ob, oc, w):
    n, d = x.shape
    return pl.pallas_call(
        _out_kernel,
        out_shape=jax.ShapeDtypeStruct((n, d), F32),
        grid=(n // TM_OUT,),
        in_specs=[
            pl.BlockSpec((TM_OUT, d), lambda i: (i, 0)),
            pl.BlockSpec((TM_OUT, GDN_WIDTH), lambda i: (i, 0)),
            pl.BlockSpec((TM_OUT, MOBA_WIDTH), lambda i: (i, 0)),
            pl.BlockSpec((TM_OUT, SCONV_WIDTH), lambda i: (i, 0)),
            pl.BlockSpec((None, d, d), lambda i: (l, 0, 0)),
        ],
        out_specs=pl.BlockSpec((TM_OUT, d), lambda i: (i, 0)),
        compiler_params=_cparams(("parallel",)),
        name="out_proj",
    )(x, oa, ob, oc, w)


def _mixer(x, l, mix_norm, w_in, w_in16, w_bc16, conv_qkv_w, a_log, dt_bias, gdn_norm_w, conv_c_w, w_out, rel_bias,
           bsz, seq):
    proj, ba = _proj(x, l, mix_norm, w_in, w_in16, w_bc16)
    o_a = _gdn(proj, ba, conv_qkv_w, a_log, dt_bias, gdn_norm_w, bsz, seq)
    o_b = _moba(*_moba_prep(proj, bsz, seq), rel_bias, bsz, seq)
    o_c = _sconv(proj, conv_c_w, bsz, seq)
    return _out(x, l, o_a, o_b, o_c, w_out)


def kernel(x, ffn1_norm, ffn1_w_gate, ffn1_w_up, ffn1_w_down, mix_norm, w_in, conv_qkv_w, a_log, dt_bias,
           gdn_norm_w, conv_c_w, w_out, ffn2_norm, ffn2_w_gate, ffn2_w_up, ffn2_w_down, rel_bias, final_norm):
    bsz, seq, d = x.shape
    x = x.reshape(bsz * seq, d)
    ffn1 = [w.astype(BF16) for w in (ffn1_w_gate, ffn1_w_up, ffn1_w_down)]
    ffn2 = [w.astype(BF16) for w in (ffn2_w_gate, ffn2_w_up, ffn2_w_down)]
    w_in16 = w_in.astype(BF16)
    w_bc16 = w_in16[..., COL_B + 2 * GDN_HEADS:]
    w_out16 = w_out.astype(BF16)
    for l in range(DEPTH):
        x = _ffn(x, l, ffn1_norm[l], *ffn1, final_norm, False)
        x = _mixer(x, l, mix_norm[l], w_in, w_in16, w_bc16, conv_qkv_w[l], a_log[l], dt_bias[l], gdn_norm_w[l],
                   conv_c_w[l], w_out16, rel_bias, bsz, seq)
        x = _ffn(x, l, ffn2_norm[l], *ffn2, final_norm, l == DEPTH - 1)
    return x.reshape(bsz, seq, d)
```

```python
import functools
import math

import jax
import jax.numpy as jnp
import numpy as np
from jax import lax
from jax.experimental import pallas as pl
from jax.experimental.pallas import tpu as pltpu

F32 = jnp.float32
BF16 = jnp.bfloat16

D_MODEL = 2048
DEPTH = 2
EPS = 1e-6
D_FF = 5632

GDN_HEAD_DIM = 128
GDN_HEADS = 8
GDN_WIDTH = GDN_HEADS * GDN_HEAD_DIM
GDN_CONV = 4
GDN_CHUNK = 64

MOBA_HEAD_DIM = 128
MOBA_HEADS = 4
MOBA_WIDTH = MOBA_HEADS * MOBA_HEAD_DIM
MOBA_BLOCK = 256
MOBA_TOPK = 3

SCONV_WIDTH = D_MODEL - GDN_WIDTH - MOBA_WIDTH
SCONV_K = 3

REL_BUCKETS = 32
REL_MAX_DIST = 128
NEG_INF = -1e30

LANES = 128
CONV_HALO = 8

PROJ_MAIN = 4 * GDN_WIDTH + 3 * MOBA_WIDTH + 3 * SCONV_WIDTH
COL_B = 4 * GDN_WIDTH
COL_C = COL_B + 3 * MOBA_WIDTH

VMEM_LIMIT = 56 * 1024 * 1024

TM_FFN = 512
TF_FFN = 512
TM_PROJ = 512
TN_PROJ = 1024
TM_OUT = 512
TB_GDN = 256
HEADS_PER_TRIP = 8
MOBA_KV_PER_STEP = 4
MASK_PAD_ROWS = 8


def _cparams(sem):
    return pltpu.CompilerParams(dimension_semantics=sem, vmem_limit_bytes=VMEM_LIMIT)


def _rms(x, w):
    return x * lax.rsqrt(jnp.mean(x * x, axis=-1, keepdims=True) + EPS) * w


def _split3(x):
    x1 = x.astype(BF16)
    r1 = x - x1.astype(F32)
    x2 = r1.astype(BF16)
    x3 = (r1 - x2.astype(F32)).astype(BF16)
    return x1, x2, x3


def _dot(a, b):
    return jnp.dot(a, b, preferred_element_type=F32)


def _dot_nt(a, b):
    return lax.dot_general(a, b, (((1,), (1,)), ((), ())), preferred_element_type=F32)


def _dot_tn(a, b):
    return lax.dot_general(a, b, (((0,), (0,)), ((), ())), preferred_element_type=F32)


def _ffn_kernel(x_ref, nw_ref, wg_ref, wu_ref, wd_ref, fw_ref, o_ref, hn_ref, acc_ref, *, n_f, final_norm):
    f = pl.program_id(1)

    @pl.when(f == 0)
    def _():
        hn_ref[...] = _rms(x_ref[...], nw_ref[...]).astype(BF16)
        acc_ref[...] = jnp.zeros_like(acc_ref)

    hn = hn_ref[...]
    g = _dot(hn, wg_ref[...])
    u = _dot(hn, wu_ref[...])
    a = (g * jax.nn.sigmoid(g) * u).astype(BF16)
    acc_ref[...] += _dot(a, wd_ref[...])

    @pl.when(f == n_f - 1)
    def _():
        y = x_ref[...] + 0.5 * acc_ref[...]
        o_ref[...] = _rms(y, fw_ref[...]) if final_norm else y


def _ffn(x, l, nw, wg, wu, wd, fw, final_norm):
    n, d = x.shape
    ff = wg.shape[2]
    n_f = ff // TF_FFN
    return pl.pallas_call(
        functools.partial(_ffn_kernel, n_f=n_f, final_norm=final_norm),
        out_shape=jax.ShapeDtypeStruct((n, d), F32),
        grid=(n // TM_FFN, n_f),
        in_specs=[
            pl.BlockSpec((TM_FFN, d), lambda i, f: (i, 0)),
            pl.BlockSpec((1, d), lambda i, f: (0, 0)),
            pl.BlockSpec((None, d, TF_FFN), lambda i, f: (l, 0, f)),
            pl.BlockSpec((None, d, TF_FFN), lambda i, f: (l, 0, f)),
            pl.BlockSpec((None, TF_FFN, d), lambda i, f: (l, f, 0)),
            pl.BlockSpec((1, d), lambda i, f: (0, 0)),
        ],
        out_specs=pl.BlockSpec((TM_FFN, d), lambda i, f: (i, 0)),
        scratch_shapes=[pltpu.VMEM((TM_FFN, d), BF16), pltpu.VMEM((TM_FFN, d), F32)],
        compiler_params=_cparams(("parallel", "arbitrary")),
        name="ffn",
    )(x, nw.reshape(1, d), wg, wu, wd, fw.reshape(1, d))


def _proj_kernel(x_ref, nw_ref, wa_ref, wbc_ref, wba_ref, o_ref, ba_ref, hn_ref, *, n_a):
    j = pl.program_id(1)

    @pl.when(j == 0)
    def _():
        h = _rms(x_ref[...], nw_ref[...])
        h1, h2, _ = _split3(h)
        hn_ref[...] = h1
        w = wba_ref[...]
        w1 = w.astype(BF16)
        w2 = (w - w1.astype(F32)).astype(BF16)
        ba_ref[...] = _dot_nt(h1, w1) + (_dot_nt(h1, w2) + _dot_nt(h2, w1))

    @pl.when(j < n_a)
    def _():
        o_ref[...] = _dot_nt(hn_ref[...], wa_ref[...])

    @pl.when(j >= n_a)
    def _():
        o_ref[...] = _dot_nt(hn_ref[...], wbc_ref[...])


def _proj(x, l, nw, w_in, w_in16, w_bc16):
    n, d = x.shape
    n_a = COL_B // TN_PROJ
    ba_blk = COL_B // LANES
    return pl.pallas_call(
        functools.partial(_proj_kernel, n_a=n_a),
        out_shape=(jax.ShapeDtypeStruct((n, PROJ_MAIN), F32), jax.ShapeDtypeStruct((n, LANES), F32)),
        grid=(n // TM_PROJ, PROJ_MAIN // TN_PROJ),
        in_specs=[
            pl.BlockSpec((TM_PROJ, d), lambda i, j: (i, 0)),
            pl.BlockSpec((1, d), lambda i, j: (0, 0)),
            pl.BlockSpec((None, TN_PROJ, d), lambda i, j: (l, jnp.minimum(j, n_a - 1), 0)),
            pl.BlockSpec((None, TN_PROJ, d), lambda i, j: (l, jnp.maximum(j - n_a, 0), 0)),
            pl.BlockSpec((None, LANES, d), lambda i, j: (l, ba_blk, 0)),
        ],
        out_specs=(pl.BlockSpec((TM_PROJ, TN_PROJ), lambda i, j: (i, j)),
                   pl.BlockSpec((TM_PROJ, LANES), lambda i, j: (i, 0))),
        scratch_shapes=[pltpu.VMEM((TM_PROJ, d), BF16)],
        compiler_params=_cparams(("parallel", "arbitrary")),
        name="proj",
    )(x, nw.reshape(1, d), w_in16, w_bc16, w_in)


def _conv_silu(x_ref, hist_ref, ext_ref, w_ref, first):
    tb = x_ref.shape[0]

    @pl.when(first)
    def _():
        hist_ref[...] = jnp.zeros_like(hist_ref)

    x = x_ref[...]
    ext_ref[0:CONV_HALO, :] = hist_ref[...]
    ext_ref[CONV_HALO:, :] = x
    hist_ref[...] = x[tb - CONV_HALO:, :]
    width = w_ref.shape[0]
    y = x * w_ref[width - 1:width, :]
    for s in range(1, width):
        y = y + ext_ref[CONV_HALO - s:CONV_HALO - s + tb, :] * w_ref[width - 1 - s:width - s, :]
    return y * jax.nn.sigmoid(y)


def _gdn_kernel(q_ref, k_ref, v_ref, z_ref, ba_ref, wq_ref, wk_ref, wv_ref, alog_ref, dtb_ref, nw_ref,
                o_ref,
                hq_ref, hk_ref, hv_ref, ext_ref, qc_ref, kc_ref, vc_ref, gc_ref, gct_ref, beta_ref,
                s_ref):
    t = pl.program_id(1)
    tb = q_ref.shape[0]
    nchunk = tb // GDN_CHUNK
    dk = GDN_HEAD_DIM
    first = t == 0

    qc_ref[...] = _conv_silu(q_ref, hq_ref, ext_ref, wq_ref, first)
    kc_ref[...] = _conv_silu(k_ref, hk_ref, ext_ref, wk_ref, first)
    vc_ref[...] = _conv_silu(v_ref, hv_ref, ext_ref, wv_ref, first)

    @pl.when(first)
    def _():
        s_ref[...] = jnp.zeros_like(s_ref)

    ba = ba_ref[...]
    beta_ref[...] = jax.nn.sigmoid(ba)
    xg = ba + dtb_ref[...]
    softplus = jnp.maximum(xg, 0.0) + jnp.log1p(jnp.exp(-jnp.abs(xg)))
    g = -jnp.exp(alog_ref[...]) * softplus
    row = lax.broadcasted_iota(jnp.int32, (tb, tb), 0)
    col = lax.broadcasted_iota(jnp.int32, (tb, tb), 1)
    shift = int(math.log2(GDN_CHUNK))
    same = jnp.right_shift(row, shift) == jnp.right_shift(col, shift)
    incl = same & (col <= row)
    strict = same & (col < row)
    tri = jnp.where(incl, 1.0, 0.0).astype(BF16)
    g1, g2, g3 = _split3(g)
    gc = _dot(tri, g1) + (_dot(tri, g2) + _dot(tri, g3))
    gc_ref[...] = gc
    gct_ref[...] = gc.T

    lane = lax.broadcasted_iota(jnp.int32, (tb, LANES), 1)
    eye = jnp.where(row == col, 1.0, 0.0)

    def head(h, s, done):
        c0 = pl.multiple_of(h * dk, dk)
        q = qc_ref[:, pl.ds(c0, dk)]
        k = kc_ref[:, pl.ds(c0, dk)]
        v = vc_ref[:, pl.ds(c0, dk)]
        z = z_ref[:, pl.ds(c0, dk)]
        q = q * lax.rsqrt(jnp.sum(q * q, axis=-1, keepdims=True) + EPS) * (dk ** -0.5)
        k = k * lax.rsqrt(jnp.sum(k * k, axis=-1, keepdims=True) + EPS)
        gcol = jnp.sum(jnp.where(lane == GDN_HEADS + h, gc_ref[...], 0.0), axis=-1, keepdims=True)
        bcol = jnp.sum(jnp.where(lane == h, beta_ref[...], 0.0), axis=-1, keepdims=True)
        grow = gct_ref[pl.ds(GDN_HEADS + h, 1), :]

        decay = jnp.where(incl, jnp.exp(jnp.where(incl, gcol - grow, 0.0)), 0.0)
        kb = k * bcol
        k16 = k.astype(BF16)
        kk = _dot_nt(kb.astype(BF16), k16)
        qk = _dot_nt(q.astype(BF16), k16)
        yield
        a_intra = (qk * decay).astype(BF16)
        m = -jnp.where(strict, kk * decay, 0.0)
        p = eye + m
        for _ in range(int(math.log2(GDN_CHUNK)) - 1):
            m16 = m.astype(BF16)
            m = _dot(m16, m16)
            yield
            p = p + _dot(p.astype(BF16), m.astype(BF16))
        eg = jnp.exp(gcol)
        rhs = jnp.concatenate([v * bcol, kb * eg], axis=1).astype(BF16)
        yield
        uw = _dot(p.astype(BF16), rhs)
        yield
        u = uw[:, :dk]
        w16 = uw[:, dk:].astype(BF16)
        qg16 = (q * eg).astype(BF16)

        zero16 = jnp.zeros((GDN_CHUNK, dk), BF16)
        outs = []
        for c in range(nchunk):
            r0, r1 = c * GDN_CHUNK, (c + 1) * GDN_CHUNK
            s16 = s.astype(BF16)
            ws = _dot(w16[r0:r1], s16)
            qs = _dot(qg16[r0:r1], s16)
            yield
            v_new = (u[r0:r1] - ws).astype(BF16)
            vn_full = jnp.concatenate([zero16] * c + [v_new] + [zero16] * (nchunk - 1 - c), axis=0)
            gl = gcol[r1 - 1:r1, :]
            kg = k[r0:r1] * jnp.exp(gl - gcol[r0:r1])
            s = s * jnp.exp(gl) + _dot_tn(kg.astype(BF16), v_new)
            outs.append(qs + _dot(a_intra[r0:r1], vn_full))
            yield
        o = jnp.concatenate(outs, axis=0)
        o = _rms(o, nw_ref[...]) * (z * jax.nn.sigmoid(z))
        done.append((o.astype(o_ref.dtype), s))

    def head_group(i, carry):
        hs = [i * HEADS_PER_TRIP + u for u in range(HEADS_PER_TRIP)]
        results = []
        live = [head(h, s_ref[h], results) for h in hs]
        while live:
            live = [g for g in live if next(g, live) is not live]
        for h, (o, s) in zip(hs, results):
            s_ref[h] = s
            o_ref[:, pl.ds(pl.multiple_of(h * dk, dk), dk)] = o
        return carry

    lax.fori_loop(0, GDN_HEADS // HEADS_PER_TRIP, head_group, 0)


def _gdn(proj, ba, conv_w, a_log, dt_bias, norm_w, bsz, seq):
    tb = TB_GDN
    nt = seq // tb
    gw = GDN_WIDTH
    alog = jnp.zeros((1, LANES), F32).at[0, GDN_HEADS:2 * GDN_HEADS].set(a_log)
    dtb = jnp.zeros((1, LANES), F32).at[0, GDN_HEADS:2 * GDN_HEADS].set(dt_bias)

    def rows(b, t):
        return b * nt + t

    return pl.pallas_call(
        _gdn_kernel,
        out_shape=jax.ShapeDtypeStruct((bsz * seq, gw), BF16),
        grid=(bsz, nt),
        in_specs=[
            pl.BlockSpec((tb, gw), lambda b, t: (rows(b, t), 0)),
            pl.BlockSpec((tb, gw), lambda b, t: (rows(b, t), 1)),
            pl.BlockSpec((tb, gw), lambda b, t: (rows(b, t), 2)),
            pl.BlockSpec((tb, gw), lambda b, t: (rows(b, t), 3)),
            pl.BlockSpec((tb, LANES), lambda b, t: (rows(b, t), 0)),
            pl.BlockSpec((GDN_CONV, gw), lambda b, t: (0, 0)),
            pl.BlockSpec((GDN_CONV, gw), lambda b, t: (0, 1)),
            pl.BlockSpec((GDN_CONV, gw), lambda b, t: (0, 2)),
            pl.BlockSpec((1, LANES), lambda b, t: (0, 0)),
            pl.BlockSpec((1, LANES), lambda b, t: (0, 0)),
            pl.BlockSpec((1, GDN_HEAD_DIM), lambda b, t: (0, 0)),
        ],
        out_specs=pl.BlockSpec((tb, gw), lambda b, t: (rows(b, t), 0)),
        scratch_shapes=[
            pltpu.VMEM((CONV_HALO, gw), F32), pltpu.VMEM((CONV_HALO, gw), F32), pltpu.VMEM((CONV_HALO, gw), F32),
            pltpu.VMEM((CONV_HALO + tb, gw), F32),
            pltpu.VMEM((tb, gw), F32), pltpu.VMEM((tb, gw), F32), pltpu.VMEM((tb, gw), F32),
            pltpu.VMEM((tb, LANES), F32), pltpu.VMEM((LANES, tb), F32), pltpu.VMEM((tb, LANES), F32),
            pltpu.VMEM((GDN_HEADS, GDN_HEAD_DIM, GDN_HEAD_DIM), F32),
        ],
        compiler_params=_cparams(("parallel", "arbitrary")),
        name="gdn",
    )(proj, proj, proj, proj, ba, conv_w, conv_w, conv_w, alog, dtb, norm_w.reshape(1, GDN_HEAD_DIM))


def _moba_prep_kernel(q_ref, k_ref, v_ref, qt_ref, k16_ref, vt_ref, km_ref):
    t = pl.program_id(1)
    qt_ref[...] = (q_ref[...] * (MOBA_HEAD_DIM ** -0.5)).T.astype(BF16)
    vt_ref[...] = v_ref[...].T.astype(BF16)
    k = k_ref[...]
    k16_ref[...] = k.astype(BF16)
    km_ref[pl.ds(t, 1), :] = jnp.mean(k, axis=0, keepdims=True)


def _moba_prep(proj, bsz, seq):
    blk = MOBA_BLOCK
    nb = seq // blk
    w = MOBA_WIDTH
    cb = COL_B // w
    return pl.pallas_call(
        _moba_prep_kernel,
        out_shape=(jax.ShapeDtypeStruct((bsz, w, seq), BF16),
                   jax.ShapeDtypeStruct((bsz * seq, w), BF16),
                   jax.ShapeDtypeStruct((bsz, w, seq), BF16),
                   jax.ShapeDtypeStruct((bsz, nb, w), F32)),
        grid=(bsz, nb),
        in_specs=[pl.BlockSpec((blk, w), lambda b, t: (b * nb + t, cb)),
                  pl.BlockSpec((blk, w), lambda b, t: (b * nb + t, cb + 1)),
                  pl.BlockSpec((blk, w), lambda b, t: (b * nb + t, cb + 2))],
        out_specs=(pl.BlockSpec((None, w, blk), lambda b, t: (b, 0, t)),
                   pl.BlockSpec((blk, w), lambda b, t: (b * nb + t, 0)),
                   pl.BlockSpec((None, w, blk), lambda b, t: (b, 0, t)),
                   pl.BlockSpec((None, nb, w), lambda b, t: (b, 0, 0))),
        compiler_params=_cparams(("parallel", "arbitrary")),
        name="moba_prep",
    )(proj, proj, proj)


def _moba_kernel(qi_ref, j0_ref, *refs):
    nkv = MOBA_KV_PER_STEP
    qt_ref = refs[0]
    k_refs = refs[1:1 + nkv]
    vt_refs = refs[1 + nkv:1 + 2 * nkv]
    km_ref, bown_ref, bprev_ref, bfar_ref, o_ref, m_ref, l_ref, acc_ref, msk_ref = refs[1 + 2 * nkv:]
    step = pl.program_id(1)
    qi = qi_ref[step]
    j0 = j0_ref[step]
    blk = MOBA_BLOCK
    hd = MOBA_HEAD_DIM
    nb = km_ref.shape[0]
    heads = range(MOBA_HEADS)

    def rows(h):
        return slice(h * hd, (h + 1) * hd)

    def mask_row(h, j):
        return msk_ref[h, pl.ds(jnp.where(j <= qi, qi - j, nb), 1), :]

    def update(bias):
        old = [(m_ref[h], l_ref[h], acc_ref[h]) for h in heads]
        new = []
        for h in heads:
            m_prev, l_prev, acc_prev = old[h]
            qt = qt_ref[rows(h), :]
            s = jnp.concatenate([_dot(k_refs[u][:, rows(h)], qt) + bias(h, u) for u in range(nkv)], axis=0)
            m_new = jnp.maximum(m_prev, jnp.max(s, axis=0, keepdims=True))
            alpha = jnp.exp(m_prev - m_new)
            p = jnp.exp(s - m_new)
            vt = jnp.concatenate([vt_refs[u][rows(h), :] for u in range(nkv)], axis=1)
            new.append((m_new, alpha * l_prev + jnp.sum(p, axis=0, keepdims=True),
                        alpha * acc_prev + _dot(vt, p.astype(BF16))))
        for h in heads:
            m_ref[h], l_ref[h], acc_ref[h] = new[h]

    @pl.when(j0 == 0)
    def _():
        blkid = lax.broadcasted_iota(jnp.int32, (nb, blk), 0)
        for h in heads:
            m_ref[h] = jnp.full((1, blk), NEG_INF, F32)
            l_ref[h] = jnp.zeros((1, blk), F32)
            acc_ref[h] = jnp.zeros((hd, blk), F32)
            k1, k2, _ = _split3(km_ref[:, rows(h)])
            qt = qt_ref[rows(h), :]
            gate = _dot(k1, qt) + _dot(k2, qt)
            gate = jnp.where(blkid < qi, gate, -jnp.inf)
            cnt = jnp.zeros((nb, blk), F32)
            for i in range(nb):
                gi = gate[i:i + 1, :]
                ahead = (gi > gate) | ((gi == gate) & (blkid > i))
                cnt = cnt + jnp.where(ahead, 1.0, 0.0)
            msk_ref[h, 0:nb, :] = jnp.where(cnt < MOBA_TOPK, 0.0, NEG_INF)
            msk_ref[h, nb:, :] = jnp.full((msk_ref.shape[1] - nb, blk), NEG_INF, F32)

        def bias(h, u):
            if u == 0:
                return bown_ref[h]
            return (bprev_ref[h] if u == 1 else bfar_ref[h]) + mask_row(h, u)

        update(bias)

    @pl.when(j0 > 0)
    def _():
        update(lambda h, u: bfar_ref[h] + mask_row(h, j0 + u))

    @pl.when(j0 + nkv > qi)
    def _():
        for h in heads:
            o_ref[:, rows(h)] = (acc_ref[h] / l_ref[h]).T.astype(o_ref.dtype)


def _rel_bucket(dist):
    n = jnp.maximum(dist, 0)
    max_exact = REL_BUCKETS // 2
    nf = jnp.maximum(n, max_exact).astype(F32)
    large = max_exact + (jnp.log(nf / max_exact) / math.log(REL_MAX_DIST / max_exact)
                         * (REL_BUCKETS - max_exact)).astype(jnp.int32)
    large = jnp.minimum(large, REL_BUCKETS - 1)
    return jnp.where(n < max_exact, n, large)


def _bias_lookup(rel_bias, dist):
    bucket = _rel_bucket(dist)[None]
    out = jnp.zeros((rel_bias.shape[1],) + dist.shape, F32)
    for b in range(REL_BUCKETS):
        out = jnp.where(bucket == b, rel_bias[b].astype(F32).reshape((-1,) + (1,) * dist.ndim), out)
    return out


def _moba(qt, k16, vt, kmean, rel_bias, bsz, seq):
    blk = MOBA_BLOCK
    nb = seq // blk
    w = MOBA_WIDTH
    assert blk >= REL_MAX_DIST
    kk = jnp.arange(blk)[:, None]
    qq = jnp.arange(blk)[None, :]
    b_own = jnp.where((qq >= kk)[None], _bias_lookup(rel_bias, qq - kk), NEG_INF)
    b_prev = _bias_lookup(rel_bias, blk + qq - kk)
    b_far = jnp.broadcast_to(_bias_lookup(rel_bias, jnp.full((1, 1), 2 * blk)), (MOBA_HEADS, 1, blk))
    nkv = MOBA_KV_PER_STEP
    steps = [(qi, j0) for qi in range(nb) for j0 in range(0, qi + 1, nkv)]
    qi_arr = jnp.asarray(np.array([s[0] for s in steps], np.int32))
    j0_arr = jnp.asarray(np.array([s[1] for s in steps], np.int32))

    def kblk(u):
        return lambda b, s, qi, j0: (b * nb + jnp.maximum(qi[s] - j0[s] - u, 0), 0)

    def vblk(u):
        return lambda b, s, qi, j0: (b, 0, jnp.maximum(qi[s] - j0[s] - u, 0))

    grid_spec = pltpu.PrefetchScalarGridSpec(
        num_scalar_prefetch=2,
        grid=(bsz, len(steps)),
        in_specs=(
            [pl.BlockSpec((None, w, blk), lambda b, s, qi, j0: (b, 0, qi[s]))]
            + [pl.BlockSpec((blk, w), kblk(u)) for u in range(nkv)]
            + [pl.BlockSpec((None, w, blk), vblk(u)) for u in range(nkv)]
            + [pl.BlockSpec((None, nb, w), lambda b, s, qi, j0: (b, 0, 0)),
               pl.BlockSpec((MOBA_HEADS, blk, blk), lambda b, s, qi, j0: (0, 0, 0)),
               pl.BlockSpec((MOBA_HEADS, blk, blk), lambda b, s, qi, j0: (0, 0, 0)),
               pl.BlockSpec((MOBA_HEADS, 1, blk), lambda b, s, qi, j0: (0, 0, 0))]),
        out_specs=pl.BlockSpec((blk, w), lambda b, s, qi, j0: (b * nb + qi[s], 0)),
        scratch_shapes=[
            pltpu.VMEM((MOBA_HEADS, 1, blk), F32),
            pltpu.VMEM((MOBA_HEADS, 1, blk), F32),
            pltpu.VMEM((MOBA_HEADS, MOBA_HEAD_DIM, blk), F32),
            pltpu.VMEM((MOBA_HEADS, nb + MASK_PAD_ROWS, blk), F32),
        ],
    )
    return pl.pallas_call(
        _moba_kernel,
        out_shape=jax.ShapeDtypeStruct((bsz * seq, w), BF16),
        grid_spec=grid_spec,
        compiler_params=_cparams(("parallel", "arbitrary")),
        name="moba",
    )(qi_arr, j0_arr, qt, *([k16] * nkv), *([vt] * nkv), kmean, b_own, b_prev, b_far)


def _sconv_kernel(x_ref, bg_ref, cg_ref, w_ref, o_ref, ext_ref):
    seq = x_ref.shape[0]
    u = cg_ref[...] * x_ref[...]
    ext_ref[0:CONV_HALO, :] = jnp.zeros((CONV_HALO, u.shape[1]), F32)
    ext_ref[CONV_HALO:, :] = u
    width = w_ref.shape[0]
    y = u * w_ref[width - 1:width, :]
    for s in range(1, width):
        y = y + ext_ref[CONV_HALO - s:CONV_HALO - s + seq, :] * w_ref[width - 1 - s:width - s, :]
    o_ref[...] = (bg_ref[...] * y).astype(o_ref.dtype)


def _sconv(proj, conv_w, bsz, seq):
    cw = SCONV_WIDTH
    nc = cw // LANES
    c0 = COL_C // LANES
    return pl.pallas_call(
        _sconv_kernel,
        out_shape=jax.ShapeDtypeStruct((bsz * seq, cw), BF16),
        grid=(bsz, nc),
        in_specs=[
            pl.BlockSpec((seq, LANES), lambda b, c: (b, c0 + c)),
            pl.BlockSpec((seq, LANES), lambda b, c: (b, c0 + nc + c)),
            pl.BlockSpec((seq, LANES), lambda b, c: (b, c0 + 2 * nc + c)),
            pl.BlockSpec((SCONV_K, LANES), lambda b, c: (0, c)),
        ],
        out_specs=pl.BlockSpec((seq, LANES), lambda b, c: (b, c)),
        scratch_shapes=[pltpu.VMEM((CONV_HALO + seq, LANES), F32)],
        compiler_params=_cparams(("parallel", "parallel")),
        name="sconv",
    )(proj, proj, proj, conv_w)


def _out_kernel(x_ref, oa_ref, ob_ref, oc_ref, w_ref, o_ref):
    a0, a1, a2 = GDN_WIDTH, GDN_WIDTH + MOBA_WIDTH, D_MODEL
    y = _dot(oa_ref[...], w_ref[0:a0, :])
    y = y + _dot(ob_ref[...], w_ref[a0:a1, :])
    y = y + _dot(oc_ref[...], w_ref[a1:a2, :])
    o_ref[...] = x_ref[...] + y


def _out(x, l, oa, ob, oc, w):
    n, d = x.shape
    return pl.pallas_call(
        _out_kernel,
        out_shape=jax.ShapeDtypeStruct((n, d), F32),
        grid=(n // TM_OUT,),
        in_specs=[
            pl.BlockSpec((TM_OUT, d), lambda i: (i, 0)),
            pl.BlockSpec((TM_OUT, GDN_WIDTH), lambda i: (i, 0)),
            pl.BlockSpec((TM_OUT, MOBA_WIDTH), lambda i: (i, 0)),
            pl.BlockSpec((TM_OUT, SCONV_WIDTH), lambda i: (i, 0)),
            pl.BlockSpec((None, d, d), lambda i: (l, 0, 0)),
        ],
        out_specs=pl.BlockSpec((TM_OUT, d), lambda i: (i, 0)),
        compiler_params=_cparams(("parallel",)),
        name="out_proj",
    )(x, oa, ob, oc, w)


def _mixer(x, l, mix_norm, w_in, w_in16, w_bc16, conv_qkv_w, a_log, dt_bias, gdn_norm_w, conv_c_w, w_out, rel_bias,
           bsz, seq):
    proj, ba = _proj(x, l, mix_norm, w_in, w_in16, w_bc16)
    o_a = _gdn(proj, ba, conv_qkv_w, a_log, dt_bias, gdn_norm_w, bsz, seq)
    o_b = _moba(*_moba_prep(proj, bsz, seq), rel_bias, bsz, seq)
    o_c = _sconv(proj, conv_c_w, bsz, seq)
    return _out(x, l, o_a, o_b, o_c, w_out)


def kernel(x, ffn1_norm, ffn1_w_gate, ffn1_w_up, ffn1_w_down, mix_norm, w_in, conv_qkv_w, a_log, dt_bias,
           gdn_norm_w, conv_c_w, w_out, ffn2_norm, ffn2_w_gate, ffn2_w_up, ffn2_w_down, rel_bias, final_norm):
    bsz, seq, d = x.shape
    x = x.reshape(bsz * seq, d)
    ffn1 = [w.astype(BF16) for w in (ffn1_w_gate, ffn1_w_up, ffn1_w_down)]
    ffn2 = [w.astype(BF16) for w in (ffn2_w_gate, ffn2_w_up, ffn2_w_down)]
    w_in = jnp.swapaxes(w_in, 1, 2)
    w_in16 = w_in.astype(BF16)
    w_bc16 = w_in16[:, COL_B + 2 * GDN_HEADS:, :]
    w_out16 = w_out.astype(BF16)
    for l in range(DEPTH):
        x = _ffn(x, l, ffn1_norm[l], *ffn1, final_norm, False)
        x = _mixer(x, l, mix_norm[l], w_in, w_in16, w_bc16, conv_qkv_w[l], a_log[l], dt_bias[l], gdn_norm_w[l],
                   conv_c_w[l], w_out16, rel_bias, bsz, seq)
        x = _ffn(x, l, ffn2_norm[l], *ffn2, final_norm, l == DEPTH - 1)
    return x.reshape(bsz, seq, d)
```

```python
import functools
import math

import jax
import jax.numpy as jnp
import numpy as np
from jax import lax
from jax.experimental import pallas as pl
from jax.experimental.pallas import tpu as pltpu

F32 = jnp.float32
BF16 = jnp.bfloat16

D_MODEL = 2048
DEPTH = 2
EPS = 1e-6
D_FF = 5632

GDN_HEAD_DIM = 128
GDN_HEADS = 8
GDN_WIDTH = GDN_HEADS * GDN_HEAD_DIM
GDN_CONV = 4
GDN_CHUNK = 64

MOBA_HEAD_DIM = 128
MOBA_HEADS = 4
MOBA_WIDTH = MOBA_HEADS * MOBA_HEAD_DIM
MOBA_BLOCK = 256
MOBA_TOPK = 3

SCONV_WIDTH = D_MODEL - GDN_WIDTH - MOBA_WIDTH
SCONV_K = 3

REL_BUCKETS = 32
REL_MAX_DIST = 128
NEG_INF = -1e30

LANES = 128
CONV_HALO = 8

PROJ_MAIN = 4 * GDN_WIDTH + 3 * MOBA_WIDTH + 3 * SCONV_WIDTH
COL_B = 4 * GDN_WIDTH
COL_C = COL_B + 3 * MOBA_WIDTH

VMEM_LIMIT = 56 * 1024 * 1024

TM_FFN = 512
TF_FFN = 512
TM_PROJ = 1024
TN_PROJ = 1024
TM_OUT = 512
TB_GDN = 256
HEADS_PER_TRIP = 8
MOBA_KV_PER_STEP = 4
MASK_PAD_ROWS = 8


def _cparams(sem):
    return pltpu.CompilerParams(dimension_semantics=sem, vmem_limit_bytes=VMEM_LIMIT)


def _rms(x, w):
    return x * lax.rsqrt(jnp.mean(x * x, axis=-1, keepdims=True) + EPS) * w


def _split3(x):
    x1 = x.astype(BF16)
    r1 = x - x1.astype(F32)
    x2 = r1.astype(BF16)
    x3 = (r1 - x2.astype(F32)).astype(BF16)
    return x1, x2, x3


def _dot(a, b):
    return jnp.dot(a, b, preferred_element_type=F32)


def _dot_nt(a, b):
    return lax.dot_general(a, b, (((1,), (1,)), ((), ())), preferred_element_type=F32)


def _dot_tn(a, b):
    return lax.dot_general(a, b, (((0,), (0,)), ((), ())), preferred_element_type=F32)


def _ffn_kernel(x_ref, nw_ref, wg_ref, wu_ref, wd_ref, fw_ref, o_ref, hn_ref, acc_ref, *, n_f, final_norm):
    f = pl.program_id(1)

    @pl.when(f == 0)
    def _():
        hn_ref[...] = _rms(x_ref[...], nw_ref[...]).astype(BF16)
        acc_ref[...] = jnp.zeros_like(acc_ref)

    hn = hn_ref[...]
    g = _dot(hn, wg_ref[...])
    u = _dot(hn, wu_ref[...])
    a = (g * jax.nn.sigmoid(g) * u).astype(BF16)
    acc_ref[...] += _dot(a, wd_ref[...])

    @pl.when(f == n_f - 1)
    def _():
        y = x_ref[...] + 0.5 * acc_ref[...]
        o_ref[...] = _rms(y, fw_ref[...]) if final_norm else y


def _ffn(x, l, nw, wg, wu, wd, fw, final_norm):
    n, d = x.shape
    ff = wg.shape[2]
    n_f = ff // TF_FFN
    return pl.pallas_call(
        functools.partial(_ffn_kernel, n_f=n_f, final_norm=final_norm),
        out_shape=jax.ShapeDtypeStruct((n, d), F32),
        grid=(n // TM_FFN, n_f),
        in_specs=[
            pl.BlockSpec((TM_FFN, d), lambda i, f: (i, 0)),
            pl.BlockSpec((1, d), lambda i, f: (0, 0)),
            pl.BlockSpec((None, d, TF_FFN), lambda i, f: (l, 0, f)),
            pl.BlockSpec((None, d, TF_FFN), lambda i, f: (l, 0, f)),
            pl.BlockSpec((None, TF_FFN, d), lambda i, f: (l, f, 0)),
            pl.BlockSpec((1, d), lambda i, f: (0, 0)),
        ],
        out_specs=pl.BlockSpec((TM_FFN, d), lambda i, f: (i, 0)),
        scratch_shapes=[pltpu.VMEM((TM_FFN, d), BF16), pltpu.VMEM((TM_FFN, d), F32)],
        compiler_params=_cparams(("parallel", "arbitrary")),
        name="ffn",
    )(x, nw.reshape(1, d), wg, wu, wd, fw.reshape(1, d))


def _proj_kernel(x_ref, nw_ref, wa_ref, wbc_ref, wba_ref, o_ref, ba_ref, hn_ref, *, n_a):
    j = pl.program_id(1)

    @pl.when(j == 0)
    def _():
        h = _rms(x_ref[...], nw_ref[...])
        h1, h2, _ = _split3(h)
        hn_ref[...] = h1
        w = wba_ref[...]
        w1 = w.astype(BF16)
        w2 = (w - w1.astype(F32)).astype(BF16)
        ba_ref[...] = _dot_nt(h1, w1) + (_dot_nt(h1, w2) + _dot_nt(h2, w1))

    @pl.when(j < n_a)
    def _():
        o_ref[...] = _dot_nt(hn_ref[...], wa_ref[...])

    @pl.when(j >= n_a)
    def _():
        o_ref[...] = _dot_nt(hn_ref[...], wbc_ref[...])


def _proj(x, l, nw, w_in, w_in16, w_bc16):
    n, d = x.shape
    n_a = COL_B // TN_PROJ
    ba_blk = COL_B // LANES
    return pl.pallas_call(
        functools.partial(_proj_kernel, n_a=n_a),
        out_shape=(jax.ShapeDtypeStruct((n, PROJ_MAIN), F32), jax.ShapeDtypeStruct((n, LANES), F32)),
        grid=(n // TM_PROJ, PROJ_MAIN // TN_PROJ),
        in_specs=[
            pl.BlockSpec((TM_PROJ, d), lambda i, j: (i, 0)),
            pl.BlockSpec((1, d), lambda i, j: (0, 0)),
            pl.BlockSpec((None, TN_PROJ, d), lambda i, j: (l, jnp.minimum(j, n_a - 1), 0)),
            pl.BlockSpec((None, TN_PROJ, d), lambda i, j: (l, jnp.maximum(j - n_a, 0), 0)),
            pl.BlockSpec((None, LANES, d), lambda i, j: (l, ba_blk, 0)),
        ],
        out_specs=(pl.BlockSpec((TM_PROJ, TN_PROJ), lambda i, j: (i, j)),
                   pl.BlockSpec((TM_PROJ, LANES), lambda i, j: (i, 0))),
        scratch_shapes=[pltpu.VMEM((TM_PROJ, d), BF16)],
        compiler_params=_cparams(("parallel", "arbitrary")),
        name="proj",
    )(x, nw.reshape(1, d), w_in16, w_bc16, w_in)


def _conv_silu(x_ref, hist_ref, ext_ref, w_ref, first):
    tb = x_ref.shape[0]

    @pl.when(first)
    def _():
        hist_ref[...] = jnp.zeros_like(hist_ref)

    x = x_ref[...]
    ext_ref[0:CONV_HALO, :] = hist_ref[...]
    ext_ref[CONV_HALO:, :] = x
    hist_ref[...] = x[tb - CONV_HALO:, :]
    width = w_ref.shape[0]
    y = x * w_ref[width - 1:width, :]
    for s in range(1, width):
        y = y + ext_ref[CONV_HALO - s:CONV_HALO - s + tb, :] * w_ref[width - 1 - s:width - s, :]
    return y * jax.nn.sigmoid(y)


def _gdn_kernel(q_ref, k_ref, v_ref, z_ref, ba_ref, wq_ref, wk_ref, wv_ref, alog_ref, dtb_ref, nw_ref,
                o_ref,
                hq_ref, hk_ref, hv_ref, ext_ref, qc_ref, kc_ref, vc_ref, gc_ref, gct_ref, beta_ref,
                s_ref):
    t = pl.program_id(1)
    tb = q_ref.shape[0]
    nchunk = tb // GDN_CHUNK
    dk = GDN_HEAD_DIM
    first = t == 0

    qc_ref[...] = _conv_silu(q_ref, hq_ref, ext_ref, wq_ref, first)
    kc_ref[...] = _conv_silu(k_ref, hk_ref, ext_ref, wk_ref, first)
    vc_ref[...] = _conv_silu(v_ref, hv_ref, ext_ref, wv_ref, first)

    @pl.when(first)
    def _():
        s_ref[...] = jnp.zeros_like(s_ref)

    ba = ba_ref[...]
    beta_ref[...] = jax.nn.sigmoid(ba)
    xg = ba + dtb_ref[...]
    softplus = jnp.maximum(xg, 0.0) + jnp.log1p(jnp.exp(-jnp.abs(xg)))
    g = -jnp.exp(alog_ref[...]) * softplus
    row = lax.broadcasted_iota(jnp.int32, (tb, tb), 0)
    col = lax.broadcasted_iota(jnp.int32, (tb, tb), 1)
    shift = int(math.log2(GDN_CHUNK))
    same = jnp.right_shift(row, shift) == jnp.right_shift(col, shift)
    incl = same & (col <= row)
    strict = same & (col < row)
    tri = jnp.where(incl, 1.0, 0.0).astype(BF16)
    g1, g2, g3 = _split3(g)
    gc = _dot(tri, g1) + (_dot(tri, g2) + _dot(tri, g3))
    gc_ref[...] = gc
    gct_ref[...] = gc.T

    lane = lax.broadcasted_iota(jnp.int32, (tb, LANES), 1)
    eye = jnp.where(row == col, 1.0, 0.0)

    def head(h, s, done):
        c0 = pl.multiple_of(h * dk, dk)
        q = qc_ref[:, pl.ds(c0, dk)]
        k = kc_ref[:, pl.ds(c0, dk)]
        v = vc_ref[:, pl.ds(c0, dk)]
        z = z_ref[:, pl.ds(c0, dk)]
        q = q * lax.rsqrt(jnp.sum(q * q, axis=-1, keepdims=True) + EPS) * (dk ** -0.5)
        k = k * lax.rsqrt(jnp.sum(k * k, axis=-1, keepdims=True) + EPS)
        gcol = jnp.sum(jnp.where(lane == GDN_HEADS + h, gc_ref[...], 0.0), axis=-1, keepdims=True)
        bcol = jnp.sum(jnp.where(lane == h, beta_ref[...], 0.0), axis=-1, keepdims=True)
        grow = gct_ref[pl.ds(GDN_HEADS + h, 1), :]

        decay = jnp.where(incl, jnp.exp(jnp.where(incl, gcol - grow, 0.0)), 0.0)
        kb = k * bcol
        k16 = k.astype(BF16)
        kk = _dot_nt(kb.astype(BF16), k16)
        qk = _dot_nt(q.astype(BF16), k16)
        yield
        a_intra = (qk * decay).astype(BF16)
        m = -jnp.where(strict, kk * decay, 0.0)
        p = eye + m
        for _ in range(int(math.log2(GDN_CHUNK)) - 1):
            m16 = m.astype(BF16)
            m = _dot(m16, m16)
            yield
            p = p + _dot(p.astype(BF16), m.astype(BF16))
        eg = jnp.exp(gcol)
        rhs = jnp.concatenate([v * bcol, kb * eg], axis=1).astype(BF16)
        yield
        uw = _dot(p.astype(BF16), rhs)
        yield
        u = uw[:, :dk]
        w16 = uw[:, dk:].astype(BF16)
        qg16 = (q * eg).astype(BF16)

        zero16 = jnp.zeros((GDN_CHUNK, dk), BF16)
        outs = []
        for c in range(nchunk):
            r0, r1 = c * GDN_CHUNK, (c + 1) * GDN_CHUNK
            s16 = s.astype(BF16)
            ws = _dot(w16[r0:r1], s16)
            qs = _dot(qg16[r0:r1], s16)
            yield
            v_new = (u[r0:r1] - ws).astype(BF16)
            vn_full = jnp.concatenate([zero16] * c + [v_new] + [zero16] * (nchunk - 1 - c), axis=0)
            gl = gcol[r1 - 1:r1, :]
            kg = k[r0:r1] * jnp.exp(gl - gcol[r0:r1])
            s = s * jnp.exp(gl) + _dot_tn(kg.astype(BF16), v_new)
            outs.append(qs + _dot(a_intra[r0:r1], vn_full))
            yield
        o = jnp.concatenate(outs, axis=0)
        o = _rms(o, nw_ref[...]) * (z * jax.nn.sigmoid(z))
        done.append((o.astype(o_ref.dtype), s))

    def head_group(i, carry):
        hs = [i * HEADS_PER_TRIP + u for u in range(HEADS_PER_TRIP)]
        results = []
        live = [head(h, s_ref[h], results) for h in hs]
        while live:
            live = [g for g in live if next(g, live) is not live]
        for h, (o, s) in zip(hs, results):
            s_ref[h] = s
            o_ref[:, pl.ds(pl.multiple_of(h * dk, dk), dk)] = o
        return carry

    lax.fori_loop(0, GDN_HEADS // HEADS_PER_TRIP, head_group, 0)


def _gdn(proj, ba, conv_w, a_log, dt_bias, norm_w, bsz, seq):
    tb = TB_GDN
    nt = seq // tb
    gw = GDN_WIDTH
    alog = jnp.zeros((1, LANES), F32).at[0, GDN_HEADS:2 * GDN_HEADS].set(a_log)
    dtb = jnp.zeros((1, LANES), F32).at[0, GDN_HEADS:2 * GDN_HEADS].set(dt_bias)

    def rows(b, t):
        return b * nt + t

    return pl.pallas_call(
        _gdn_kernel,
        out_shape=jax.ShapeDtypeStruct((bsz * seq, gw), BF16),
        grid=(bsz, nt),
        in_specs=[
            pl.BlockSpec((tb, gw), lambda b, t: (rows(b, t), 0)),
            pl.BlockSpec((tb, gw), lambda b, t: (rows(b, t), 1)),
            pl.BlockSpec((tb, gw), lambda b, t: (rows(b, t), 2)),
            pl.BlockSpec((tb, gw), lambda b, t: (rows(b, t), 3)),
            pl.BlockSpec((tb, LANES), lambda b, t: (rows(b, t), 0)),
            pl.BlockSpec((GDN_CONV, gw), lambda b, t: (0, 0)),
            pl.BlockSpec((GDN_CONV, gw), lambda b, t: (0, 1)),
            pl.BlockSpec((GDN_CONV, gw), lambda b, t: (0, 2)),
            pl.BlockSpec((1, LANES), lambda b, t: (0, 0)),
            pl.BlockSpec((1, LANES), lambda b, t: (0, 0)),
            pl.BlockSpec((1, GDN_HEAD_DIM), lambda b, t: (0, 0)),
        ],
        out_specs=pl.BlockSpec((tb, gw), lambda b, t: (rows(b, t), 0)),
        scratch_shapes=[
            pltpu.VMEM((CONV_HALO, gw), F32), pltpu.VMEM((CONV_HALO, gw), F32), pltpu.VMEM((CONV_HALO, gw), F32),
            pltpu.VMEM((CONV_HALO + tb, gw), F32),
            pltpu.VMEM((tb, gw), F32), pltpu.VMEM((tb, gw), F32), pltpu.VMEM((tb, gw), F32),
            pltpu.VMEM((tb, LANES), F32), pltpu.VMEM((LANES, tb), F32), pltpu.VMEM((tb, LANES), F32),
            pltpu.VMEM((GDN_HEADS, GDN_HEAD_DIM, GDN_HEAD_DIM), F32),
        ],
        compiler_params=_cparams(("parallel", "arbitrary")),
        name="gdn",
    )(proj, proj, proj, proj, ba, conv_w, conv_w, conv_w, alog, dtb, norm_w.reshape(1, GDN_HEAD_DIM))


def _moba_prep_kernel(q_ref, k_ref, v_ref, qt_ref, k16_ref, vt_ref, km_ref):
    t = pl.program_id(1)
    qt_ref[...] = (q_ref[...] * (MOBA_HEAD_DIM ** -0.5)).T.astype(BF16)
    vt_ref[...] = v_ref[...].T.astype(BF16)
    k = k_ref[...]
    k16_ref[...] = k.astype(BF16)
    km_ref[pl.ds(t, 1), :] = jnp.mean(k, axis=0, keepdims=True)


def _moba_prep(proj, bsz, seq):
    blk = MOBA_BLOCK
    nb = seq // blk
    w = MOBA_WIDTH
    cb = COL_B // w
    return pl.pallas_call(
        _moba_prep_kernel,
        out_shape=(jax.ShapeDtypeStruct((bsz, w, seq), BF16),
                   jax.ShapeDtypeStruct((bsz * seq, w), BF16),
                   jax.ShapeDtypeStruct((bsz, w, seq), BF16),
                   jax.ShapeDtypeStruct((bsz, nb, w), F32)),
        grid=(bsz, nb),
        in_specs=[pl.BlockSpec((blk, w), lambda b, t: (b * nb + t, cb)),
                  pl.BlockSpec((blk, w), lambda b, t: (b * nb + t, cb + 1)),
                  pl.BlockSpec((blk, w), lambda b, t: (b * nb + t, cb + 2))],
        out_specs=(pl.BlockSpec((None, w, blk), lambda b, t: (b, 0, t)),
                   pl.BlockSpec((blk, w), lambda b, t: (b * nb + t, 0)),
                   pl.BlockSpec((None, w, blk), lambda b, t: (b, 0, t)),
                   pl.BlockSpec((None, nb, w), lambda b, t: (b, 0, 0))),
        compiler_params=_cparams(("parallel", "arbitrary")),
        name="moba_prep",
    )(proj, proj, proj)


def _moba_kernel(qi_ref, j0_ref, *refs):
    nkv = MOBA_KV_PER_STEP
    qt_ref = refs[0]
    k_refs = refs[1:1 + nkv]
    vt_refs = refs[1 + nkv:1 + 2 * nkv]
    km_ref, bown_ref, bprev_ref, bfar_ref, o_ref, m_ref, l_ref, acc_ref, msk_ref = refs[1 + 2 * nkv:]
    step = pl.program_id(1)
    qi = qi_ref[step]
    j0 = j0_ref[step]
    blk = MOBA_BLOCK
    hd = MOBA_HEAD_DIM
    nb = km_ref.shape[0]
    heads = range(MOBA_HEADS)

    def rows(h):
        return slice(h * hd, (h + 1) * hd)

    def mask_row(h, j):
        return msk_ref[h, pl.ds(jnp.where(j <= qi, qi - j, nb), 1), :]

    def update(bias):
        old = [(m_ref[h], l_ref[h], acc_ref[h]) for h in heads]
        new = []
        for h in heads:
            m_prev, l_prev, acc_prev = old[h]
            qt = qt_ref[rows(h), :]
            s = jnp.concatenate([_dot(k_refs[u][:, rows(h)], qt) + bias(h, u) for u in range(nkv)], axis=0)
            m_new = jnp.maximum(m_prev, jnp.max(s, axis=0, keepdims=True))
            alpha = jnp.exp(m_prev - m_new)
            p = jnp.exp(s - m_new)
            vt = jnp.concatenate([vt_refs[u][rows(h), :] for u in range(nkv)], axis=1)
            new.append((m_new, alpha * l_prev + jnp.sum(p, axis=0, keepdims=True),
                        alpha * acc_prev + _dot(vt, p.astype(BF16))))
        for h in heads:
            m_ref[h], l_ref[h], acc_ref[h] = new[h]

    @pl.when(j0 == 0)
    def _():
        blkid = lax.broadcasted_iota(jnp.int32, (nb, blk), 0)
        for h in heads:
            m_ref[h] = jnp.full((1, blk), NEG_INF, F32)
            l_ref[h] = jnp.zeros((1, blk), F32)
            acc_ref[h] = jnp.zeros((hd, blk), F32)
            k1, k2, _ = _split3(km_ref[:, rows(h)])
            qt = qt_ref[rows(h), :]
            gate = _dot(k1, qt) + _dot(k2, qt)
            gate = jnp.where(blkid < qi, gate, -jnp.inf)
            cnt = jnp.zeros((nb, blk), F32)
            for i in range(nb):
                gi = gate[i:i + 1, :]
                ahead = (gi > gate) | ((gi == gate) & (blkid > i))
                cnt = cnt + jnp.where(ahead, 1.0, 0.0)
            msk_ref[h, 0:nb, :] = jnp.where(cnt < MOBA_TOPK, 0.0, NEG_INF)
            msk_ref[h, nb:, :] = jnp.full((msk_ref.shape[1] - nb, blk), NEG_INF, F32)

        def bias(h, u):
            if u == 0:
                return bown_ref[h]
            return (bprev_ref[h] if u == 1 else bfar_ref[h]) + mask_row(h, u)

        update(bias)

    @pl.when(j0 > 0)
    def _():
        update(lambda h, u: bfar_ref[h] + mask_row(h, j0 + u))

    @pl.when(j0 + nkv > qi)
    def _():
        for h in heads:
            o_ref[:, rows(h)] = (acc_ref[h] / l_ref[h]).T.astype(o_ref.dtype)


def _rel_bucket(dist):
    n = jnp.maximum(dist, 0)
    max_exact = REL_BUCKETS // 2
    nf = jnp.maximum(n, max_exact).astype(F32)
    large = max_exact + (jnp.log(nf / max_exact) / math.log(REL_MAX_DIST / max_exact)
                         * (REL_BUCKETS - max_exact)).astype(jnp.int32)
    large = jnp.minimum(large, REL_BUCKETS - 1)
    return jnp.where(n < max_exact, n, large)


def _bias_lookup(rel_bias, dist):
    bucket = _rel_bucket(dist)[None]
    out = jnp.zeros((rel_bias.shape[1],) + dist.shape, F32)
    for b in range(REL_BUCKETS):
        out = jnp.where(bucket == b, rel_bias[b].astype(F32).reshape((-1,) + (1,) * dist.ndim), out)
    return out


def _moba(qt, k16, vt, kmean, rel_bias, bsz, seq):
    blk = MOBA_BLOCK
    nb = seq // blk
    w = MOBA_WIDTH
    assert blk >= REL_MAX_DIST
    kk = jnp.arange(blk)[:, None]
    qq = jnp.arange(blk)[None, :]
    b_own = jnp.where((qq >= kk)[None], _bias_lookup(rel_bias, qq - kk), NEG_INF)
    b_prev = _bias_lookup(rel_bias, blk + qq - kk)
    b_far = jnp.broadcast_to(_bias_lookup(rel_bias, jnp.full((1, 1), 2 * blk)), (MOBA_HEADS, 1, blk))
    nkv = MOBA_KV_PER_STEP
    steps = [(qi, j0) for qi in range(nb) for j0 in range(0, qi + 1, nkv)]
    qi_arr = jnp.asarray(np.array([s[0] for s in steps], np.int32))
    j0_arr = jnp.asarray(np.array([s[1] for s in steps], np.int32))

    def kblk(u):
        return lambda b, s, qi, j0: (b * nb + jnp.maximum(qi[s] - j0[s] - u, 0), 0)

    def vblk(u):
        return lambda b, s, qi, j0: (b, 0, jnp.maximum(qi[s] - j0[s] - u, 0))

    grid_spec = pltpu.PrefetchScalarGridSpec(
        num_scalar_prefetch=2,
        grid=(bsz, len(steps)),
        in_specs=(
            [pl.BlockSpec((None, w, blk), lambda b, s, qi, j0: (b, 0, qi[s]))]
            + [pl.BlockSpec((blk, w), kblk(u)) for u in range(nkv)]
            + [pl.BlockSpec((None, w, blk), vblk(u)) for u in range(nkv)]
            + [pl.BlockSpec((None, nb, w), lambda b, s, qi, j0: (b, 0, 0)),
               pl.BlockSpec((MOBA_HEADS, blk, blk), lambda b, s, qi, j0: (0, 0, 0)),
               pl.BlockSpec((MOBA_HEADS, blk, blk), lambda b, s, qi, j0: (0, 0, 0)),
               pl.BlockSpec((MOBA_HEADS, 1, blk), lambda b, s, qi, j0: (0, 0, 0))]),
        out_specs=pl.BlockSpec((blk, w), lambda b, s, qi, j0: (b * nb + qi[s], 0)),
        scratch_shapes=[
            pltpu.VMEM((MOBA_HEADS, 1, blk), F32),
            pltpu.VMEM((MOBA_HEADS, 1, blk), F32),
            pltpu.VMEM((MOBA_HEADS, MOBA_HEAD_DIM, blk), F32),
            pltpu.VMEM((MOBA_HEADS, nb + MASK_PAD_ROWS, blk), F32),
        ],
    )
    return pl.pallas_call(
        _moba_kernel,
        out_shape=jax.ShapeDtypeStruct((bsz * seq, w), BF16),
        grid_spec=grid_spec,
        compiler_params=_cparams(("parallel", "arbitrary")),
        name="moba",
    )(qi_arr, j0_arr, qt, *([k16] * nkv), *([vt] * nkv), kmean, b_own, b_prev, b_far)


def _sconv_kernel(x_ref, bg_ref, cg_ref, w_ref, o_ref, ext_ref):
    seq = x_ref.shape[0]
    u = cg_ref[...] * x_ref[...]
    ext_ref[0:CONV_HALO, :] = jnp.zeros((CONV_HALO, u.shape[1]), F32)
    ext_ref[CONV_HALO:, :] = u
    width = w_ref.shape[0]
    y = u * w_ref[width - 1:width, :]
    for s in range(1, width):
        y = y + ext_ref[CONV_HALO - s:CONV_HALO - s + seq, :] * w_ref[width - 1 - s:width - s, :]
    o_ref[...] = (bg_ref[...] * y).astype(o_ref.dtype)


def _sconv(proj, conv_w, bsz, seq):
    cw = SCONV_WIDTH
    nc = cw // LANES
    c0 = COL_C // LANES
    return pl.pallas_call(
        _sconv_kernel,
        out_shape=jax.ShapeDtypeStruct((bsz * seq, cw), BF16),
        grid=(bsz, nc),
        in_specs=[
            pl.BlockSpec((seq, LANES), lambda b, c: (b, c0 + c)),
            pl.BlockSpec((seq, LANES), lambda b, c: (b, c0 + nc + c)),
            pl.BlockSpec((seq, LANES), lambda b, c: (b, c0 + 2 * nc + c)),
            pl.BlockSpec((SCONV_K, LANES), lambda b, c: (0, c)),
        ],
        out_specs=pl.BlockSpec((seq, LANES), lambda b, c: (b, c)),
        scratch_shapes=[pltpu.VMEM((CONV_HALO + seq, LANES), F32)],
        compiler_params=_cparams(("parallel", "parallel")),
        name="sconv",
    )(proj, proj, proj, conv_w)


def _out_kernel(x_ref, oa_ref, ob_ref, oc_ref, w_ref, o_ref):
    a0, a1, a2 = GDN_WIDTH, GDN_WIDTH + MOBA_WIDTH, D_MODEL
    y = _dot(oa_ref[...], w_ref[0:a0, :])
    y = y + _dot(ob_ref[...], w_ref[a0:a1, :])
    y = y + _dot(oc_ref[...], w_ref[a1:a2, :])
    o_ref[...] = x_ref[...] + y


def _out(x, l, oa, ob, oc, w):
    n, d = x.shape
    return pl.pallas_call(
        _out_kernel,
        out_shape=jax.ShapeDtypeStruct((n, d), F32),
        grid=(n // TM_OUT,),
        in_specs=[
            pl.BlockSpec((TM_OUT, d), lambda i: (i, 0)),
            pl.BlockSpec((TM_OUT, GDN_WIDTH), lambda i: (i, 0)),
            pl.BlockSpec((TM_OUT, MOBA_WIDTH), lambda i: (i, 0)),
            pl.BlockSpec((TM_OUT, SCONV_WIDTH), lambda i: (i, 0)),
            pl.BlockSpec((None, d, d), lambda i: (l, 0, 0)),
        ],
        out_specs=pl.BlockSpec((TM_OUT, d), lambda i: (i, 0)),
        compiler_params=_cparams(("parallel",)),
        name="out_proj",
    )(x, oa, ob, oc, w)


def _mixer(x, l, mix_norm, w_in, w_in16, w_bc16, conv_qkv_w, a_log, dt_bias, gdn_norm_w, conv_c_w, w_out, rel_bias,
           bsz, seq):
    proj, ba = _proj(x, l, mix_norm, w_in, w_in16, w_bc16)
    o_a = _gdn(proj, ba, conv_qkv_w, a_log, dt_bias, gdn_norm_w, bsz, seq)
    o_b = _moba(*_moba_prep(proj, bsz, seq), rel_bias, bsz, seq)
    o_c = _sconv(proj, conv_c_w, bsz, seq)
    return _out(x, l, o_a, o_b, o_c, w_out)


def kernel(x, ffn1_norm, ffn1_w_gate, ffn1_w_up, ffn1_w_down, mix_norm, w_in, conv_qkv_w, a_log, dt_bias,
           gdn_norm_w, conv_c_w, w_out, ffn2_norm, ffn2_w_gate, ffn2_w_up, ffn2_w_down, rel_bias, final_norm):
    bsz, seq, d = x.shape
    x = x.reshape(bsz * seq, d)
    ffn1 = [w.astype(BF16) for w in (ffn1_w_gate, ffn1_w_up, ffn1_w_down)]
    ffn2 = [w.astype(BF16) for w in (ffn2_w_gate, ffn2_w_up, ffn2_w_down)]
    w_in = jnp.swapaxes(w_in, 1, 2)
    w_in16 = w_in.astype(BF16)
    w_bc16 = w_in16[:, COL_B + 2 * GDN_HEADS:, :]
    w_out16 = w_out.astype(BF16)
    for l in range(DEPTH):
        x = _ffn(x, l, ffn1_norm[l], *ffn1, final_norm, False)
        x = _mixer(x, l, mix_norm[l], w_in, w_in16, w_bc16, conv_qkv_w[l], a_log[l], dt_bias[l], gdn_norm_w[l],
                   conv_c_w[l], w_out16, rel_bias, bsz, seq)
        x = _ffn(x, l, ffn2_norm[l], *ffn2, final_norm, l == DEPTH - 1)
    return x.reshape(bsz, seq, d)
```

```python
import functools
import math

import jax
import jax.numpy as jnp
import numpy as np
from jax import lax
from jax.experimental import pallas as pl
from jax.experimental.pallas import tpu as pltpu

F32 = jnp.float32
BF16 = jnp.bfloat16

D_MODEL = 2048
DEPTH = 2
EPS = 1e-6
D_FF = 5632

GDN_HEAD_DIM = 128
GDN_HEADS = 8
GDN_WIDTH = GDN_HEADS * GDN_HEAD_DIM
GDN_CONV = 4
GDN_CHUNK = 64

MOBA_HEAD_DIM = 128
MOBA_HEADS = 4
MOBA_WIDTH = MOBA_HEADS * MOBA_HEAD_DIM
MOBA_BLOCK = 256
MOBA_TOPK = 3

SCONV_WIDTH = D_MODEL - GDN_WIDTH - MOBA_WIDTH
SCONV_K = 3

REL_BUCKETS = 32
REL_MAX_DIST = 128
NEG_INF = -1e30

LANES = 128
CONV_HALO = 8

PROJ_MAIN = 4 * GDN_WIDTH + 3 * MOBA_WIDTH + 3 * SCONV_WIDTH
COL_B = 4 * GDN_WIDTH
COL_C = COL_B + 3 * MOBA_WIDTH

VMEM_LIMIT = 56 * 1024 * 1024

TM_FFN = 512
TF_FFN = 512
TM_PROJ = 1024
TN_PROJ = 1024
TM_OUT = 512
TB_GDN = 256
HEADS_PER_TRIP = 8
MOBA_KV_PER_STEP = 4
MASK_PAD_ROWS = 8


def _cparams(sem):
    return pltpu.CompilerParams(dimension_semantics=sem, vmem_limit_bytes=VMEM_LIMIT)


def _rms(x, w):
    return x * lax.rsqrt(jnp.mean(x * x, axis=-1, keepdims=True) + EPS) * w


def _split3(x):
    x1 = x.astype(BF16)
    r1 = x - x1.astype(F32)
    x2 = r1.astype(BF16)
    x3 = (r1 - x2.astype(F32)).astype(BF16)
    return x1, x2, x3


def _dot(a, b):
    return jnp.dot(a, b, preferred_element_type=F32)


def _dot_nt(a, b):
    return lax.dot_general(a, b, (((1,), (1,)), ((), ())), preferred_element_type=F32)


def _dot_tn(a, b):
    return lax.dot_general(a, b, (((0,), (0,)), ((), ())), preferred_element_type=F32)


def _ffn_kernel(x_ref, nw_ref, wg_ref, wu_ref, wd_ref, fw_ref, o_ref, hn_ref, acc_ref, *, n_f, final_norm):
    f = pl.program_id(1)

    @pl.when(f == 0)
    def _():
        hn_ref[...] = _rms(x_ref[...], nw_ref[...]).astype(BF16)
        acc_ref[...] = jnp.zeros_like(acc_ref)

    hn = hn_ref[...]
    g = _dot(hn, wg_ref[...])
    u = _dot(hn, wu_ref[...])
    a = (g * jax.nn.sigmoid(g) * u).astype(BF16)
    acc_ref[...] += _dot(a, wd_ref[...].astype(BF16))

    @pl.when(f == n_f - 1)
    def _():
        y = x_ref[...] + 0.5 * acc_ref[...]
        o_ref[...] = _rms(y, fw_ref[...]) if final_norm else y


def _ffn(x, l, nw, wg, wu, wd, fw, final_norm):
    n, d = x.shape
    ff = wg.shape[2]
    n_f = ff // TF_FFN
    return pl.pallas_call(
        functools.partial(_ffn_kernel, n_f=n_f, final_norm=final_norm),
        out_shape=jax.ShapeDtypeStruct((n, d), F32),
        grid=(n // TM_FFN, n_f),
        in_specs=[
            pl.BlockSpec((TM_FFN, d), lambda i, f: (i, 0)),
            pl.BlockSpec((1, d), lambda i, f: (0, 0)),
            pl.BlockSpec((None, d, TF_FFN), lambda i, f: (l, 0, f)),
            pl.BlockSpec((None, d, TF_FFN), lambda i, f: (l, 0, f)),
            pl.BlockSpec((None, TF_FFN, d), lambda i, f: (l, f, 0)),
            pl.BlockSpec((1, d), lambda i, f: (0, 0)),
        ],
        out_specs=pl.BlockSpec((TM_FFN, d), lambda i, f: (i, 0)),
        scratch_shapes=[pltpu.VMEM((TM_FFN, d), BF16), pltpu.VMEM((TM_FFN, d), F32)],
        compiler_params=_cparams(("parallel", "arbitrary")),
        name="ffn",
    )(x, nw.reshape(1, d), wg, wu, wd, fw.reshape(1, d))


def _proj_kernel(x_ref, nw_ref, wa_ref, wbc_ref, wba_ref, o_ref, ba_ref, hn_ref, *, n_a):
    j = pl.program_id(1)

    @pl.when(j == 0)
    def _():
        h = _rms(x_ref[...], nw_ref[...])
        h1, h2, _ = _split3(h)
        hn_ref[...] = h1
        w = wba_ref[...]
        w1 = w.astype(BF16)
        w2 = (w - w1.astype(F32)).astype(BF16)
        ba_ref[...] = _dot_nt(h1, w1) + (_dot_nt(h1, w2) + _dot_nt(h2, w1))

    @pl.when(j < n_a)
    def _():
        o_ref[...] = _dot_nt(hn_ref[...], wa_ref[...])

    @pl.when(j >= n_a)
    def _():
        o_ref[...] = _dot_nt(hn_ref[...], wbc_ref[...])


def _proj(x, l, nw, w_in, w_in16, w_bc16):
    n, d = x.shape
    n_a = COL_B // TN_PROJ
    ba_blk = COL_B // LANES
    return pl.pallas_call(
        functools.partial(_proj_kernel, n_a=n_a),
        out_shape=(jax.ShapeDtypeStruct((n, PROJ_MAIN), F32), jax.ShapeDtypeStruct((n, LANES), F32)),
        grid=(n // TM_PROJ, PROJ_MAIN // TN_PROJ),
        in_specs=[
            pl.BlockSpec((TM_PROJ, d), lambda i, j: (i, 0)),
            pl.BlockSpec((1, d), lambda i, j: (0, 0)),
            pl.BlockSpec((None, TN_PROJ, d), lambda i, j: (l, jnp.minimum(j, n_a - 1), 0)),
            pl.BlockSpec((None, TN_PROJ, d), lambda i, j: (l, jnp.maximum(j - n_a, 0), 0)),
            pl.BlockSpec((None, LANES, d), lambda i, j: (l, ba_blk, 0)),
        ],
        out_specs=(pl.BlockSpec((TM_PROJ, TN_PROJ), lambda i, j: (i, j)),
                   pl.BlockSpec((TM_PROJ, LANES), lambda i, j: (i, 0))),
        scratch_shapes=[pltpu.VMEM((TM_PROJ, d), BF16)],
        compiler_params=_cparams(("parallel", "arbitrary")),
        name="proj",
    )(x, nw.reshape(1, d), w_in16, w_bc16, w_in)


def _conv_silu(x_ref, hist_ref, ext_ref, w_ref, first):
    tb = x_ref.shape[0]

    @pl.when(first)
    def _():
        hist_ref[...] = jnp.zeros_like(hist_ref)

    x = x_ref[...]
    ext_ref[0:CONV_HALO, :] = hist_ref[...]
    ext_ref[CONV_HALO:, :] = x
    hist_ref[...] = x[tb - CONV_HALO:, :]
    width = w_ref.shape[0]
    y = x * w_ref[width - 1:width, :]
    for s in range(1, width):
        y = y + ext_ref[CONV_HALO - s:CONV_HALO - s + tb, :] * w_ref[width - 1 - s:width - s, :]
    return y * jax.nn.sigmoid(y)


def _gdn_kernel(q_ref, k_ref, v_ref, z_ref, ba_ref, wq_ref, wk_ref, wv_ref, alog_ref, dtb_ref, nw_ref,
                o_ref,
                hq_ref, hk_ref, hv_ref, ext_ref, qc_ref, kc_ref, vc_ref, gc_ref, gct_ref, beta_ref,
                s_ref):
    t = pl.program_id(1)
    tb = q_ref.shape[0]
    nchunk = tb // GDN_CHUNK
    dk = GDN_HEAD_DIM
    first = t == 0

    qc_ref[...] = _conv_silu(q_ref, hq_ref, ext_ref, wq_ref, first)
    kc_ref[...] = _conv_silu(k_ref, hk_ref, ext_ref, wk_ref, first)
    vc_ref[...] = _conv_silu(v_ref, hv_ref, ext_ref, wv_ref, first)

    @pl.when(first)
    def _():
        s_ref[...] = jnp.zeros_like(s_ref)

    ba = ba_ref[...]
    beta_ref[...] = jax.nn.sigmoid(ba)
    xg = ba + dtb_ref[...]
    softplus = jnp.maximum(xg, 0.0) + jnp.log1p(jnp.exp(-jnp.abs(xg)))
    g = -jnp.exp(alog_ref[...]) * softplus
    row = lax.broadcasted_iota(jnp.int32, (tb, tb), 0)
    col = lax.broadcasted_iota(jnp.int32, (tb, tb), 1)
    shift = int(math.log2(GDN_CHUNK))
    same = jnp.right_shift(row, shift) == jnp.right_shift(col, shift)
    incl = same & (col <= row)
    strict = same & (col < row)
    tri = jnp.where(incl, 1.0, 0.0).astype(BF16)
    g1, g2, g3 = _split3(g)
    gc = _dot(tri, g1) + (_dot(tri, g2) + _dot(tri, g3))
    gc_ref[...] = gc
    gct_ref[...] = gc.T

    lane = lax.broadcasted_iota(jnp.int32, (tb, LANES), 1)
    eye = jnp.where(row == col, 1.0, 0.0)

    def head(h, s, done):
        c0 = pl.multiple_of(h * dk, dk)
        q = qc_ref[:, pl.ds(c0, dk)]
        k = kc_ref[:, pl.ds(c0, dk)]
        v = vc_ref[:, pl.ds(c0, dk)]
        z = z_ref[:, pl.ds(c0, dk)]
        q = q * lax.rsqrt(jnp.sum(q * q, axis=-1, keepdims=True) + EPS) * (dk ** -0.5)
        k = k * lax.rsqrt(jnp.sum(k * k, axis=-1, keepdims=True) + EPS)
        gcol = jnp.sum(jnp.where(lane == GDN_HEADS + h, gc_ref[...], 0.0), axis=-1, keepdims=True)
        bcol = jnp.sum(jnp.where(lane == h, beta_ref[...], 0.0), axis=-1, keepdims=True)
        grow = gct_ref[pl.ds(GDN_HEADS + h, 1), :]

        decay = jnp.where(incl, jnp.exp(jnp.where(incl, gcol - grow, 0.0)), 0.0)
        kb = k * bcol
        k16 = k.astype(BF16)
        kk = _dot_nt(kb.astype(BF16), k16)
        qk = _dot_nt(q.astype(BF16), k16)
        yield
        a_intra = (qk * decay).astype(BF16)
        m = -jnp.where(strict, kk * decay, 0.0)
        p = eye + m
        for _ in range(int(math.log2(GDN_CHUNK)) - 1):
            m16 = m.astype(BF16)
            m = _dot(m16, m16)
            yield
            p = p + _dot(p.astype(BF16), m.astype(BF16))
        eg = jnp.exp(gcol)
        rhs = jnp.concatenate([v * bcol, kb * eg], axis=1).astype(BF16)
        yield
        uw = _dot(p.astype(BF16), rhs)
        yield
        u = uw[:, :dk]
        w16 = uw[:, dk:].astype(BF16)
        qg16 = (q * eg).astype(BF16)

        zero16 = jnp.zeros((GDN_CHUNK, dk), BF16)
        outs = []
        for c in range(nchunk):
            r0, r1 = c * GDN_CHUNK, (c + 1) * GDN_CHUNK
            s16 = s.astype(BF16)
            ws = _dot(w16[r0:r1], s16)
            qs = _dot(qg16[r0:r1], s16)
            yield
            v_new = (u[r0:r1] - ws).astype(BF16)
            vn_full = jnp.concatenate([zero16] * c + [v_new] + [zero16] * (nchunk - 1 - c), axis=0)
            gl = gcol[r1 - 1:r1, :]
            kg = k[r0:r1] * jnp.exp(gl - gcol[r0:r1])
            s = s * jnp.exp(gl) + _dot_tn(kg.astype(BF16), v_new)
            outs.append(qs + _dot(a_intra[r0:r1], vn_full))
            yield
        o = jnp.concatenate(outs, axis=0)
        o = _rms(o, nw_ref[...]) * (z * jax.nn.sigmoid(z))
        done.append((o.astype(o_ref.dtype), s))

    def head_group(i, carry):
        hs = [i * HEADS_PER_TRIP + u for u in range(HEADS_PER_TRIP)]
        results = []
        live = [head(h, s_ref[h], results) for h in hs]
        while live:
            live = [g for g in live if next(g, live) is not live]
        for h, (o, s) in zip(hs, results):
            s_ref[h] = s
            o_ref[:, pl.ds(pl.multiple_of(h * dk, dk), dk)] = o
        return carry

    lax.fori_loop(0, GDN_HEADS // HEADS_PER_TRIP, head_group, 0)


def _gdn(proj, ba, conv_w, a_log, dt_bias, norm_w, bsz, seq):
    tb = TB_GDN
    nt = seq // tb
    gw = GDN_WIDTH
    alog = jnp.zeros((1, LANES), F32).at[0, GDN_HEADS:2 * GDN_HEADS].set(a_log)
    dtb = jnp.zeros((1, LANES), F32).at[0, GDN_HEADS:2 * GDN_HEADS].set(dt_bias)

    def rows(b, t):
        return b * nt + t

    return pl.pallas_call(
        _gdn_kernel,
        out_shape=jax.ShapeDtypeStruct((bsz * seq, gw), BF16),
        grid=(bsz, nt),
        in_specs=[
            pl.BlockSpec((tb, gw), lambda b, t: (rows(b, t), 0)),
            pl.BlockSpec((tb, gw), lambda b, t: (rows(b, t), 1)),
            pl.BlockSpec((tb, gw), lambda b, t: (rows(b, t), 2)),
            pl.BlockSpec((tb, gw), lambda b, t: (rows(b, t), 3)),
            pl.BlockSpec((tb, LANES), lambda b, t: (rows(b, t), 0)),
            pl.BlockSpec((GDN_CONV, gw), lambda b, t: (0, 0)),
            pl.BlockSpec((GDN_CONV, gw), lambda b, t: (0, 1)),
            pl.BlockSpec((GDN_CONV, gw), lambda b, t: (0, 2)),
            pl.BlockSpec((1, LANES), lambda b, t: (0, 0)),
            pl.BlockSpec((1, LANES), lambda b, t: (0, 0)),
            pl.BlockSpec((1, GDN_HEAD_DIM), lambda b, t: (0, 0)),
        ],
        out_specs=pl.BlockSpec((tb, gw), lambda b, t: (rows(b, t), 0)),
        scratch_shapes=[
            pltpu.VMEM((CONV_HALO, gw), F32), pltpu.VMEM((CONV_HALO, gw), F32), pltpu.VMEM((CONV_HALO, gw), F32),
            pltpu.VMEM((CONV_HALO + tb, gw), F32),
            pltpu.VMEM((tb, gw), F32), pltpu.VMEM((tb, gw), F32), pltpu.VMEM((tb, gw), F32),
            pltpu.VMEM((tb, LANES), F32), pltpu.VMEM((LANES, tb), F32), pltpu.VMEM((tb, LANES), F32),
            pltpu.VMEM((GDN_HEADS, GDN_HEAD_DIM, GDN_HEAD_DIM), F32),
        ],
        compiler_params=_cparams(("parallel", "arbitrary")),
        name="gdn",
    )(proj, proj, proj, proj, ba, conv_w, conv_w, conv_w, alog, dtb, norm_w.reshape(1, GDN_HEAD_DIM))


def _moba_prep_kernel(q_ref, k_ref, v_ref, qt_ref, k16_ref, vt_ref, km_ref):
    t = pl.program_id(1)
    qt_ref[...] = (q_ref[...] * (MOBA_HEAD_DIM ** -0.5)).T.astype(BF16)
    vt_ref[...] = v_ref[...].T.astype(BF16)
    k = k_ref[...]
    k16_ref[...] = k.astype(BF16)
    km_ref[pl.ds(t, 1), :] = jnp.mean(k, axis=0, keepdims=True)


def _moba_prep(proj, bsz, seq):
    blk = MOBA_BLOCK
    nb = seq // blk
    w = MOBA_WIDTH
    cb = COL_B // w
    return pl.pallas_call(
        _moba_prep_kernel,
        out_shape=(jax.ShapeDtypeStruct((bsz, w, seq), BF16),
                   jax.ShapeDtypeStruct((bsz * seq, w), BF16),
                   jax.ShapeDtypeStruct((bsz, w, seq), BF16),
                   jax.ShapeDtypeStruct((bsz, nb, w), F32)),
        grid=(bsz, nb),
        in_specs=[pl.BlockSpec((blk, w), lambda b, t: (b * nb + t, cb)),
                  pl.BlockSpec((blk, w), lambda b, t: (b * nb + t, cb + 1)),
                  pl.BlockSpec((blk, w), lambda b, t: (b * nb + t, cb + 2))],
        out_specs=(pl.BlockSpec((None, w, blk), lambda b, t: (b, 0, t)),
                   pl.BlockSpec((blk, w), lambda b, t: (b * nb + t, 0)),
                   pl.BlockSpec((None, w, blk), lambda b, t: (b, 0, t)),
                   pl.BlockSpec((None, nb, w), lambda b, t: (b, 0, 0))),
        compiler_params=_cparams(("parallel", "arbitrary")),
        name="moba_prep",
    )(proj, proj, proj)


def _moba_kernel(qi_ref, j0_ref, *refs):
    nkv = MOBA_KV_PER_STEP
    qt_ref = refs[0]
    k_refs = refs[1:1 + nkv]
    vt_refs = refs[1 + nkv:1 + 2 * nkv]
    km_ref, bown_ref, bprev_ref, bfar_ref, o_ref, m_ref, l_ref, acc_ref, msk_ref = refs[1 + 2 * nkv:]
    step = pl.program_id(1)
    qi = qi_ref[step]
    j0 = j0_ref[step]
    blk = MOBA_BLOCK
    hd = MOBA_HEAD_DIM
    nb = km_ref.shape[0]
    heads = range(MOBA_HEADS)

    def rows(h):
        return slice(h * hd, (h + 1) * hd)

    def mask_row(h, j):
        return msk_ref[h, pl.ds(jnp.where(j <= qi, qi - j, nb), 1), :]

    def update(bias):
        old = [(m_ref[h], l_ref[h], acc_ref[h]) for h in heads]
        new = []
        for h in heads:
            m_prev, l_prev, acc_prev = old[h]
            qt = qt_ref[rows(h), :]
            s = jnp.concatenate([_dot(k_refs[u][:, rows(h)], qt) + bias(h, u) for u in range(nkv)], axis=0)
            m_new = jnp.maximum(m_prev, jnp.max(s, axis=0, keepdims=True))
            alpha = jnp.exp(m_prev - m_new)
            p = jnp.exp(s - m_new)
            vt = jnp.concatenate([vt_refs[u][rows(h), :] for u in range(nkv)], axis=1)
            new.append((m_new, alpha * l_prev + jnp.sum(p, axis=0, keepdims=True),
                        alpha * acc_prev + _dot(vt, p.astype(BF16))))
        for h in heads:
            m_ref[h], l_ref[h], acc_ref[h] = new[h]

    @pl.when(j0 == 0)
    def _():
        blkid = lax.broadcasted_iota(jnp.int32, (nb, blk), 0)
        for h in heads:
            m_ref[h] = jnp.full((1, blk), NEG_INF, F32)
            l_ref[h] = jnp.zeros((1, blk), F32)
            acc_ref[h] = jnp.zeros((hd, blk), F32)
            k1, k2, _ = _split3(km_ref[:, rows(h)])
            qt = qt_ref[rows(h), :]
            gate = _dot(k1, qt) + _dot(k2, qt)
            gate = jnp.where(blkid < qi, gate, -jnp.inf)
            cnt = jnp.zeros((nb, blk), F32)
            for i in range(nb):
                gi = gate[i:i + 1, :]
                ahead = (gi > gate) | ((gi == gate) & (blkid > i))
                cnt = cnt + jnp.where(ahead, 1.0, 0.0)
            msk_ref[h, 0:nb, :] = jnp.where(cnt < MOBA_TOPK, 0.0, NEG_INF)
            msk_ref[h, nb:, :] = jnp.full((msk_ref.shape[1] - nb, blk), NEG_INF, F32)

        def bias(h, u):
            if u == 0:
                return bown_ref[h]
            return (bprev_ref[h] if u == 1 else bfar_ref[h]) + mask_row(h, u)

        update(bias)

    @pl.when(j0 > 0)
    def _():
        update(lambda h, u: bfar_ref[h] + mask_row(h, j0 + u))

    @pl.when(j0 + nkv > qi)
    def _():
        for h in heads:
            o_ref[:, rows(h)] = (acc_ref[h] / l_ref[h]).T.astype(o_ref.dtype)


def _rel_bucket(dist):
    n = jnp.maximum(dist, 0)
    max_exact = REL_BUCKETS // 2
    nf = jnp.maximum(n, max_exact).astype(F32)
    large = max_exact + (jnp.log(nf / max_exact) / math.log(REL_MAX_DIST / max_exact)
                         * (REL_BUCKETS - max_exact)).astype(jnp.int32)
    large = jnp.minimum(large, REL_BUCKETS - 1)
    return jnp.where(n < max_exact, n, large)


def _bias_lookup(rel_bias, dist):
    bucket = _rel_bucket(dist)[None]
    out = jnp.zeros((rel_bias.shape[1],) + dist.shape, F32)
    for b in range(REL_BUCKETS):
        out = jnp.where(bucket == b, rel_bias[b].astype(F32).reshape((-1,) + (1,) * dist.ndim), out)
    return out


def _moba(qt, k16, vt, kmean, rel_bias, bsz, seq):
    blk = MOBA_BLOCK
    nb = seq // blk
    w = MOBA_WIDTH
    assert blk >= REL_MAX_DIST
    kk = jnp.arange(blk)[:, None]
    qq = jnp.arange(blk)[None, :]
    b_own = jnp.where((qq >= kk)[None], _bias_lookup(rel_bias, qq - kk), NEG_INF)
    b_prev = _bias_lookup(rel_bias, blk + qq - kk)
    b_far = jnp.broadcast_to(_bias_lookup(rel_bias, jnp.full((1, 1), 2 * blk)), (MOBA_HEADS, 1, blk))
    nkv = MOBA_KV_PER_STEP
    steps = [(qi, j0) for qi in range(nb) for j0 in range(0, qi + 1, nkv)]
    qi_arr = jnp.asarray(np.array([s[0] for s in steps], np.int32))
    j0_arr = jnp.asarray(np.array([s[1] for s in steps], np.int32))

    def kblk(u):
        return lambda b, s, qi, j0: (b * nb + jnp.maximum(qi[s] - j0[s] - u, 0), 0)

    def vblk(u):
        return lambda b, s, qi, j0: (b, 0, jnp.maximum(qi[s] - j0[s] - u, 0))

    grid_spec = pltpu.PrefetchScalarGridSpec(
        num_scalar_prefetch=2,
        grid=(bsz, len(steps)),
        in_specs=(
            [pl.BlockSpec((None, w, blk), lambda b, s, qi, j0: (b, 0, qi[s]))]
            + [pl.BlockSpec((blk, w), kblk(u)) for u in range(nkv)]
            + [pl.BlockSpec((None, w, blk), vblk(u)) for u in range(nkv)]
            + [pl.BlockSpec((None, nb, w), lambda b, s, qi, j0: (b, 0, 0)),
               pl.BlockSpec((MOBA_HEADS, blk, blk), lambda b, s, qi, j0: (0, 0, 0)),
               pl.BlockSpec((MOBA_HEADS, blk, blk), lambda b, s, qi, j0: (0, 0, 0)),
               pl.BlockSpec((MOBA_HEADS, 1, blk), lambda b, s, qi, j0: (0, 0, 0))]),
        out_specs=pl.BlockSpec((blk, w), lambda b, s, qi, j0: (b * nb + qi[s], 0)),
        scratch_shapes=[
            pltpu.VMEM((MOBA_HEADS, 1, blk), F32),
            pltpu.VMEM((MOBA_HEADS, 1, blk), F32),
            pltpu.VMEM((MOBA_HEADS, MOBA_HEAD_DIM, blk), F32),
            pltpu.VMEM((MOBA_HEADS, nb + MASK_PAD_ROWS, blk), F32),
        ],
    )
    return pl.pallas_call(
        _moba_kernel,
        out_shape=jax.ShapeDtypeStruct((bsz * seq, w), BF16),
        grid_spec=grid_spec,
        compiler_params=_cparams(("parallel", "arbitrary")),
        name="moba",
    )(qi_arr, j0_arr, qt, *([k16] * nkv), *([vt] * nkv), kmean, b_own, b_prev, b_far)


def _sconv_kernel(x_ref, bg_ref, cg_ref, w_ref, o_ref, ext_ref):
    seq = x_ref.shape[0]
    u = cg_ref[...] * x_ref[...]
    ext_ref[0:CONV_HALO, :] = jnp.zeros((CONV_HALO, u.shape[1]), F32)
    ext_ref[CONV_HALO:, :] = u
    width = w_ref.shape[0]
    y = u * w_ref[width - 1:width, :]
    for s in range(1, width):
        y = y + ext_ref[CONV_HALO - s:CONV_HALO - s + seq, :] * w_ref[width - 1 - s:width - s, :]
    o_ref[...] = (bg_ref[...] * y).astype(o_ref.dtype)


def _sconv(proj, conv_w, bsz, seq):
    cw = SCONV_WIDTH
    nc = cw // LANES
    c0 = COL_C // LANES
    return pl.pallas_call(
        _sconv_kernel,
        out_shape=jax.ShapeDtypeStruct((bsz * seq, cw), BF16),
        grid=(bsz, nc),
        in_specs=[
            pl.BlockSpec((seq, LANES), lambda b, c: (b, c0 + c)),
            pl.BlockSpec((seq, LANES), lambda b, c: (b, c0 + nc + c)),
            pl.BlockSpec((seq, LANES), lambda b, c: (b, c0 + 2 * nc + c)),
            pl.BlockSpec((SCONV_K, LANES), lambda b, c: (0, c)),
        ],
        out_specs=pl.BlockSpec((seq, LANES), lambda b, c: (b, c)),
        scratch_shapes=[pltpu.VMEM((CONV_HALO + seq, LANES), F32)],
        compiler_params=_cparams(("parallel", "parallel")),
        name="sconv",
    )(proj, proj, proj, conv_w)


def _out_kernel(x_ref, oa_ref, ob_ref, oc_ref, w_ref, o_ref):
    a0, a1, a2 = GDN_WIDTH, GDN_WIDTH + MOBA_WIDTH, D_MODEL
    y = _dot(oa_ref[...], w_ref[0:a0, :])
    y = y + _dot(ob_ref[...], w_ref[a0:a1, :])
    y = y + _dot(oc_ref[...], w_ref[a1:a2, :])
    o_ref[...] = x_ref[...] + y


def _out(x, l, oa, ob, oc, w):
    n, d = x.shape
    return pl.pallas_call(
        _out_kernel,
        out_shape=jax.ShapeDtypeStruct((n, d), F32),
        grid=(n // TM_OUT,),
        in_specs=[
            pl.BlockSpec((TM_OUT, d), lambda i: (i, 0)),
            pl.BlockSpec((TM_OUT, GDN_WIDTH), lambda i: (i, 0)),
            pl.BlockSpec((TM_OUT, MOBA_WIDTH), lambda i: (i, 0)),
            pl.BlockSpec((TM_OUT, SCONV_WIDTH), lambda i: (i, 0)),
            pl.BlockSpec((None, d, d), lambda i: (l, 0, 0)),
        ],
        out_specs=pl.BlockSpec((TM_OUT, d), lambda i: (i, 0)),
        compiler_params=_cparams(("parallel",)),
        name="out_proj",
    )(x, oa, ob, oc, w)


def _mixer(x, l, mix_norm, w_in, w_in16, w_bc16, conv_qkv_w, a_log, dt_bias, gdn_norm_w, conv_c_w, w_out, rel_bias,
           bsz, seq):
    proj, ba = _proj(x, l, mix_norm, w_in, w_in16, w_bc16)
    o_a = _gdn(proj, ba, conv_qkv_w, a_log, dt_bias, gdn_norm_w, bsz, seq)
    o_b = _moba(*_moba_prep(proj, bsz, seq), rel_bias, bsz, seq)
    o_c = _sconv(proj, conv_c_w, bsz, seq)
    return _out(x, l, o_a, o_b, o_c, w_out)


def kernel(x, ffn1_norm, ffn1_w_gate, ffn1_w_up, ffn1_w_down, mix_norm, w_in, conv_qkv_w, a_log, dt_bias,
           gdn_norm_w, conv_c_w, w_out, ffn2_norm, ffn2_w_gate, ffn2_w_up, ffn2_w_down, rel_bias, final_norm):
    bsz, seq, d = x.shape
    x = x.reshape(bsz * seq, d)
    ffn1 = [ffn1_w_gate.astype(BF16), ffn1_w_up.astype(BF16), ffn1_w_down]
    ffn2 = [ffn2_w_gate.astype(BF16), ffn2_w_up.astype(BF16), ffn2_w_down]
    w_in = jnp.swapaxes(w_in, 1, 2)
    w_in16 = w_in.astype(BF16)
    w_bc16 = w_in16[:, COL_B + 2 * GDN_HEADS:, :]
    w_out16 = w_out.astype(BF16)
    for l in range(DEPTH):
        x = _ffn(x, l, ffn1_norm[l], *ffn1, final_norm, False)
        x = _mixer(x, l, mix_norm[l], w_in, w_in16, w_bc16, conv_qkv_w[l], a_log[l], dt_bias[l], gdn_norm_w[l],
                   conv_c_w[l], w_out16, rel_bias, bsz, seq)
        x = _ffn(x, l, ffn2_norm[l], *ffn2, final_norm, l == DEPTH - 1)
    return x.reshape(bsz, seq, d)
```

```python
import functools
import math

import jax
import jax.numpy as jnp
import numpy as np
from jax import lax
from jax.experimental import pallas as pl
from jax.experimental.pallas import tpu as pltpu

F32 = jnp.float32
BF16 = jnp.bfloat16

D_MODEL = 2048
DEPTH = 2
EPS = 1e-6
D_FF = 5632

GDN_HEAD_DIM = 128
GDN_HEADS = 8
GDN_WIDTH = GDN_HEADS * GDN_HEAD_DIM
GDN_CONV = 4
GDN_CHUNK = 64

MOBA_HEAD_DIM = 128
MOBA_HEADS = 4
MOBA_WIDTH = MOBA_HEADS * MOBA_HEAD_DIM
MOBA_BLOCK = 256
MOBA_TOPK = 3

SCONV_WIDTH = D_MODEL - GDN_WIDTH - MOBA_WIDTH
SCONV_K = 3

REL_BUCKETS = 32
REL_MAX_DIST = 128
NEG_INF = -1e30

LANES = 128
CONV_HALO = 8

PROJ_MAIN = 4 * GDN_WIDTH + 3 * MOBA_WIDTH + 3 * SCONV_WIDTH
COL_B = 4 * GDN_WIDTH
COL_C = COL_B + 3 * MOBA_WIDTH

VMEM_LIMIT = 56 * 1024 * 1024

TM_FFN = 512
TF_FFN = 512
TM_PROJ = 1024
TN_PROJ = 1024
TB_GDN = 256
HEADS_PER_TRIP = 8
MOBA_KV_PER_STEP = 4
MASK_PAD_ROWS = 8


def _cparams(sem):
    return pltpu.CompilerParams(dimension_semantics=sem, vmem_limit_bytes=VMEM_LIMIT)


def _rms(x, w):
    return x * lax.rsqrt(jnp.mean(x * x, axis=-1, keepdims=True) + EPS) * w


def _split3(x):
    x1 = x.astype(BF16)
    r1 = x - x1.astype(F32)
    x2 = r1.astype(BF16)
    x3 = (r1 - x2.astype(F32)).astype(BF16)
    return x1, x2, x3


def _dot(a, b):
    return jnp.dot(a, b, preferred_element_type=F32)


def _dot_nt(a, b):
    return lax.dot_general(a, b, (((1,), (1,)), ((), ())), preferred_element_type=F32)


def _dot_tn(a, b):
    return lax.dot_general(a, b, (((0,), (0,)), ((), ())), preferred_element_type=F32)


def _ffn_kernel(x_ref, nw_ref, wg_ref, wu_ref, wd_ref, fw_ref, *rest, n_f, final_norm, mix):
    o_ref, hn_ref, acc_ref = rest[-3:]
    f = pl.program_id(1)

    @pl.when(f == 0)
    def _():
        x = x_ref[...]
        if mix:
            oa_ref, ob_ref, oc_ref, wo_ref = rest[:4]
            a0, a1, a2 = GDN_WIDTH, GDN_WIDTH + MOBA_WIDTH, D_MODEL
            x = x + _dot(oa_ref[...], wo_ref[0:a0, :])
            x = x + _dot(ob_ref[...], wo_ref[a0:a1, :])
            x = x + _dot(oc_ref[...], wo_ref[a1:a2, :])
        o_ref[...] = x
        hn_ref[...] = _rms(x, nw_ref[...]).astype(BF16)
        acc_ref[...] = jnp.zeros_like(acc_ref)

    hn = hn_ref[...]
    g = _dot(hn, wg_ref[...])
    u = _dot(hn, wu_ref[...])
    a = (g * jax.nn.sigmoid(g) * u).astype(BF16)
    acc_ref[...] += _dot(a, wd_ref[...])

    @pl.when(f == n_f - 1)
    def _():
        y = o_ref[...] + 0.5 * acc_ref[...]
        o_ref[...] = _rms(y, fw_ref[...]) if final_norm else y


def _ffn(x, l, nw, wg, wu, wd, fw, final_norm, mixer_out=None):
    n, d = x.shape
    ff = wg.shape[2]
    n_f = ff // TF_FFN
    in_specs = [
        pl.BlockSpec((TM_FFN, d), lambda i, f: (i, 0)),
        pl.BlockSpec((1, d), lambda i, f: (0, 0)),
        pl.BlockSpec((None, d, TF_FFN), lambda i, f: (l, 0, f)),
        pl.BlockSpec((None, d, TF_FFN), lambda i, f: (l, 0, f)),
        pl.BlockSpec((None, TF_FFN, d), lambda i, f: (l, f, 0)),
        pl.BlockSpec((1, d), lambda i, f: (0, 0)),
    ]
    args = [x, nw.reshape(1, d), wg, wu, wd, fw.reshape(1, d)]
    if mixer_out is not None:
        in_specs += [
            pl.BlockSpec((TM_FFN, GDN_WIDTH), lambda i, f: (i, 0)),
            pl.BlockSpec((TM_FFN, MOBA_WIDTH), lambda i, f: (i, 0)),
            pl.BlockSpec((TM_FFN, SCONV_WIDTH), lambda i, f: (i, 0)),
            pl.BlockSpec((None, d, d), lambda i, f: (l, 0, 0), pipeline_mode=pl.Buffered(1)),
        ]
        args += list(mixer_out)
    return pl.pallas_call(
        functools.partial(_ffn_kernel, n_f=n_f, final_norm=final_norm, mix=mixer_out is not None),
        out_shape=jax.ShapeDtypeStruct((n, d), F32),
        grid=(n // TM_FFN, n_f),
        in_specs=in_specs,
        out_specs=pl.BlockSpec((TM_FFN, d), lambda i, f: (i, 0)),
        scratch_shapes=[pltpu.VMEM((TM_FFN, d), BF16), pltpu.VMEM((TM_FFN, d), F32)],
        compiler_params=_cparams(("parallel", "arbitrary")),
        name="ffn",
    )(*args)


def _proj_kernel(x_ref, nw_ref, wa_ref, wbc_ref, wba_ref, o_ref, ba_ref, hn_ref, *, n_a):
    j = pl.program_id(1)

    @pl.when(j == 0)
    def _():
        h = _rms(x_ref[...], nw_ref[...])
        h1, h2, _ = _split3(h)
        hn_ref[...] = h1
        w = wba_ref[...]
        w1 = w.astype(BF16)
        w2 = (w - w1.astype(F32)).astype(BF16)
        ba_ref[...] = _dot_nt(h1, w1) + (_dot_nt(h1, w2) + _dot_nt(h2, w1))

    @pl.when(j < n_a)
    def _():
        o_ref[...] = _dot_nt(hn_ref[...], wa_ref[...])

    @pl.when(j >= n_a)
    def _():
        o_ref[...] = _dot_nt(hn_ref[...], wbc_ref[...])


def _proj(x, l, nw, w_in, w_in16, w_bc16):
    n, d = x.shape
    n_a = COL_B // TN_PROJ
    ba_blk = COL_B // LANES
    return pl.pallas_call(
        functools.partial(_proj_kernel, n_a=n_a),
        out_shape=(jax.ShapeDtypeStruct((n, PROJ_MAIN), F32), jax.ShapeDtypeStruct((n, LANES), F32)),
        grid=(n // TM_PROJ, PROJ_MAIN // TN_PROJ),
        in_specs=[
            pl.BlockSpec((TM_PROJ, d), lambda i, j: (i, 0)),
            pl.BlockSpec((1, d), lambda i, j: (0, 0)),
            pl.BlockSpec((None, TN_PROJ, d), lambda i, j: (l, jnp.minimum(j, n_a - 1), 0)),
            pl.BlockSpec((None, TN_PROJ, d), lambda i, j: (l, jnp.maximum(j - n_a, 0), 0)),
            pl.BlockSpec((None, LANES, d), lambda i, j: (l, ba_blk, 0)),
        ],
        out_specs=(pl.BlockSpec((TM_PROJ, TN_PROJ), lambda i, j: (i, j)),
                   pl.BlockSpec((TM_PROJ, LANES), lambda i, j: (i, 0))),
        scratch_shapes=[pltpu.VMEM((TM_PROJ, d), BF16)],
        compiler_params=_cparams(("parallel", "arbitrary")),
        name="proj",
    )(x, nw.reshape(1, d), w_in16, w_bc16, w_in)


def _conv_silu(x_ref, hist_ref, ext_ref, w_ref, first):
    tb = x_ref.shape[0]

    @pl.when(first)
    def _():
        hist_ref[...] = jnp.zeros_like(hist_ref)

    x = x_ref[...]
    ext_ref[0:CONV_HALO, :] = hist_ref[...]
    ext_ref[CONV_HALO:, :] = x
    hist_ref[...] = x[tb - CONV_HALO:, :]
    width = w_ref.shape[0]
    y = x * w_ref[width - 1:width, :]
    for s in range(1, width):
        y = y + ext_ref[CONV_HALO - s:CONV_HALO - s + tb, :] * w_ref[width - 1 - s:width - s, :]
    return y * jax.nn.sigmoid(y)


def _gdn_kernel(q_ref, k_ref, v_ref, z_ref, ba_ref, wq_ref, wk_ref, wv_ref, alog_ref, dtb_ref, nw_ref,
                o_ref,
                hq_ref, hk_ref, hv_ref, ext_ref, qc_ref, kc_ref, vc_ref, gc_ref, gct_ref, beta_ref,
                s_ref):
    t = pl.program_id(1)
    tb = q_ref.shape[0]
    nchunk = tb // GDN_CHUNK
    dk = GDN_HEAD_DIM
    first = t == 0

    qc_ref[...] = _conv_silu(q_ref, hq_ref, ext_ref, wq_ref, first)
    kc_ref[...] = _conv_silu(k_ref, hk_ref, ext_ref, wk_ref, first)
    vc_ref[...] = _conv_silu(v_ref, hv_ref, ext_ref, wv_ref, first)

    @pl.when(first)
    def _():
        s_ref[...] = jnp.zeros_like(s_ref)

    ba = ba_ref[...]
    beta_ref[...] = jax.nn.sigmoid(ba)
    xg = ba + dtb_ref[...]
    softplus = jnp.maximum(xg, 0.0) + jnp.log1p(jnp.exp(-jnp.abs(xg)))
    g = -jnp.exp(alog_ref[...]) * softplus
    row = lax.broadcasted_iota(jnp.int32, (tb, tb), 0)
    col = lax.broadcasted_iota(jnp.int32, (tb, tb), 1)
    shift = int(math.log2(GDN_CHUNK))
    same = jnp.right_shift(row, shift) == jnp.right_shift(col, shift)
    incl = same & (col <= row)
    strict = same & (col < row)
    tri = jnp.where(incl, 1.0, 0.0).astype(BF16)
    g1, g2, g3 = _split3(g)
    gc = _dot(tri, g1) + (_dot(tri, g2) + _dot(tri, g3))
    gc_ref[...] = gc
    gct_ref[...] = gc.T

    lane = lax.broadcasted_iota(jnp.int32, (tb, LANES), 1)
    eye = jnp.where(row == col, 1.0, 0.0)

    def head(h, s, done):
        c0 = pl.multiple_of(h * dk, dk)
        q = qc_ref[:, pl.ds(c0, dk)]
        k = kc_ref[:, pl.ds(c0, dk)]
        v = vc_ref[:, pl.ds(c0, dk)]
        z = z_ref[:, pl.ds(c0, dk)]
        q = q * lax.rsqrt(jnp.sum(q * q, axis=-1, keepdims=True) + EPS) * (dk ** -0.5)
        k = k * lax.rsqrt(jnp.sum(k * k, axis=-1, keepdims=True) + EPS)
        gcol = jnp.sum(jnp.where(lane == GDN_HEADS + h, gc_ref[...], 0.0), axis=-1, keepdims=True)
        bcol = jnp.sum(jnp.where(lane == h, beta_ref[...], 0.0), axis=-1, keepdims=True)
        grow = gct_ref[pl.ds(GDN_HEADS + h, 1), :]

        decay = jnp.where(incl, jnp.exp(jnp.where(incl, gcol - grow, 0.0)), 0.0)
        kb = k * bcol
        k16 = k.astype(BF16)
        kk = _dot_nt(kb.astype(BF16), k16)
        qk = _dot_nt(q.astype(BF16), k16)
        yield
        a_intra = (qk * decay).astype(BF16)
        m = -jnp.where(strict, kk * decay, 0.0)
        p = eye + m
        for _ in range(int(math.log2(GDN_CHUNK)) - 1):
            m16 = m.astype(BF16)
            m = _dot(m16, m16)
            yield
            p = p + _dot(p.astype(BF16), m.astype(BF16))
        eg = jnp.exp(gcol)
        rhs = jnp.concatenate([v * bcol, kb * eg], axis=1).astype(BF16)
        yield
        uw = _dot(p.astype(BF16), rhs)
        yield
        u = uw[:, :dk]
        w16 = uw[:, dk:].astype(BF16)
        qg16 = (q * eg).astype(BF16)

        zero16 = jnp.zeros((GDN_CHUNK, dk), BF16)
        outs = []
        for c in range(nchunk):
            r0, r1 = c * GDN_CHUNK, (c + 1) * GDN_CHUNK
            s16 = s.astype(BF16)
            ws = _dot(w16[r0:r1], s16)
            qs = _dot(qg16[r0:r1], s16)
            yield
            v_new = (u[r0:r1] - ws).astype(BF16)
            vn_full = jnp.concatenate([zero16] * c + [v_new] + [zero16] * (nchunk - 1 - c), axis=0)
            gl = gcol[r1 - 1:r1, :]
            kg = k[r0:r1] * jnp.exp(gl - gcol[r0:r1])
            s = s * jnp.exp(gl) + _dot_tn(kg.astype(BF16), v_new)
            outs.append(qs + _dot(a_intra[r0:r1], vn_full))
            yield
        o = jnp.concatenate(outs, axis=0)
        o = _rms(o, nw_ref[...]) * (z * jax.nn.sigmoid(z))
        done.append((o.astype(o_ref.dtype), s))

    def head_group(i, carry):
        hs = [i * HEADS_PER_TRIP + u for u in range(HEADS_PER_TRIP)]
        results = []
        live = [head(h, s_ref[h], results) for h in hs]
        while live:
            live = [g for g in live if next(g, live) is not live]
        for h, (o, s) in zip(hs, results):
            s_ref[h] = s
            o_ref[:, pl.ds(pl.multiple_of(h * dk, dk), dk)] = o
        return carry

    lax.fori_loop(0, GDN_HEADS // HEADS_PER_TRIP, head_group, 0)


def _gdn(proj, ba, conv_w, a_log, dt_bias, norm_w, bsz, seq):
    tb = TB_GDN
    nt = seq // tb
    gw = GDN_WIDTH
    alog = jnp.zeros((1, LANES), F32).at[0, GDN_HEADS:2 * GDN_HEADS].set(a_log)
    dtb = jnp.zeros((1, LANES), F32).at[0, GDN_HEADS:2 * GDN_HEADS].set(dt_bias)

    def rows(b, t):
        return b * nt + t

    return pl.pallas_call(
        _gdn_kernel,
        out_shape=jax.ShapeDtypeStruct((bsz * seq, gw), BF16),
        grid=(bsz, nt),
        in_specs=[
            pl.BlockSpec((tb, gw), lambda b, t: (rows(b, t), 0)),
            pl.BlockSpec((tb, gw), lambda b, t: (rows(b, t), 1)),
            pl.BlockSpec((tb, gw), lambda b, t: (rows(b, t), 2)),
            pl.BlockSpec((tb, gw), lambda b, t: (rows(b, t), 3)),
            pl.BlockSpec((tb, LANES), lambda b, t: (rows(b, t), 0)),
            pl.BlockSpec((GDN_CONV, gw), lambda b, t: (0, 0)),
            pl.BlockSpec((GDN_CONV, gw), lambda b, t: (0, 1)),
            pl.BlockSpec((GDN_CONV, gw), lambda b, t: (0, 2)),
            pl.BlockSpec((1, LANES), lambda b, t: (0, 0)),
            pl.BlockSpec((1, LANES), lambda b, t: (0, 0)),
            pl.BlockSpec((1, GDN_HEAD_DIM), lambda b, t: (0, 0)),
        ],
        out_specs=pl.BlockSpec((tb, gw), lambda b, t: (rows(b, t), 0)),
        scratch_shapes=[
            pltpu.VMEM((CONV_HALO, gw), F32), pltpu.VMEM((CONV_HALO, gw), F32), pltpu.VMEM((CONV_HALO, gw), F32),
            pltpu.VMEM((CONV_HALO + tb, gw), F32),
            pltpu.VMEM((tb, gw), F32), pltpu.VMEM((tb, gw), F32), pltpu.VMEM((tb, gw), F32),
            pltpu.VMEM((tb, LANES), F32), pltpu.VMEM((LANES, tb), F32), pltpu.VMEM((tb, LANES), F32),
            pltpu.VMEM((GDN_HEADS, GDN_HEAD_DIM, GDN_HEAD_DIM), F32),
        ],
        compiler_params=_cparams(("parallel", "arbitrary")),
        name="gdn",
    )(proj, proj, proj, proj, ba, conv_w, conv_w, conv_w, alog, dtb, norm_w.reshape(1, GDN_HEAD_DIM))


def _moba_prep_kernel(q_ref, k_ref, v_ref, qt_ref, k16_ref, vt_ref, km_ref):
    t = pl.program_id(1)
    qt_ref[...] = (q_ref[...] * (MOBA_HEAD_DIM ** -0.5)).T.astype(BF16)
    vt_ref[...] = v_ref[...].T.astype(BF16)
    k = k_ref[...]
    k16_ref[...] = k.astype(BF16)
    km_ref[pl.ds(t, 1), :] = jnp.mean(k, axis=0, keepdims=True)


def _moba_prep(proj, bsz, seq):
    blk = MOBA_BLOCK
    nb = seq // blk
    w = MOBA_WIDTH
    cb = COL_B // w
    return pl.pallas_call(
        _moba_prep_kernel,
        out_shape=(jax.ShapeDtypeStruct((bsz, w, seq), BF16),
                   jax.ShapeDtypeStruct((bsz * seq, w), BF16),
                   jax.ShapeDtypeStruct((bsz, w, seq), BF16),
                   jax.ShapeDtypeStruct((bsz, nb, w), F32)),
        grid=(bsz, nb),
        in_specs=[pl.BlockSpec((blk, w), lambda b, t: (b * nb + t, cb)),
                  pl.BlockSpec((blk, w), lambda b, t: (b * nb + t, cb + 1)),
                  pl.BlockSpec((blk, w), lambda b, t: (b * nb + t, cb + 2))],
        out_specs=(pl.BlockSpec((None, w, blk), lambda b, t: (b, 0, t)),
                   pl.BlockSpec((blk, w), lambda b, t: (b * nb + t, 0)),
                   pl.BlockSpec((None, w, blk), lambda b, t: (b, 0, t)),
                   pl.BlockSpec((None, nb, w), lambda b, t: (b, 0, 0))),
        compiler_params=_cparams(("parallel", "arbitrary")),
        name="moba_prep",
    )(proj, proj, proj)


def _moba_kernel(qi_ref, j0_ref, *refs):
    nkv = MOBA_KV_PER_STEP
    qt_ref = refs[0]
    k_refs = refs[1:1 + nkv]
    vt_refs = refs[1 + nkv:1 + 2 * nkv]
    km_ref, bown_ref, bprev_ref, bfar_ref, o_ref, m_ref, l_ref, acc_ref, msk_ref = refs[1 + 2 * nkv:]
    step = pl.program_id(1)
    qi = qi_ref[step]
    j0 = j0_ref[step]
    blk = MOBA_BLOCK
    hd = MOBA_HEAD_DIM
    nb = km_ref.shape[0]
    heads = range(MOBA_HEADS)

    def rows(h):
        return slice(h * hd, (h + 1) * hd)

    def mask_row(h, j):
        return msk_ref[h, pl.ds(jnp.where(j <= qi, qi - j, nb), 1), :]

    def update(bias):
        old = [(m_ref[h], l_ref[h], acc_ref[h]) for h in heads]
        new = []
        for h in heads:
            m_prev, l_prev, acc_prev = old[h]
            qt = qt_ref[rows(h), :]
            s = jnp.concatenate([_dot(k_refs[u][:, rows(h)], qt) + bias(h, u) for u in range(nkv)], axis=0)
            m_new = jnp.maximum(m_prev, jnp.max(s, axis=0, keepdims=True))
            alpha = jnp.exp(m_prev - m_new)
            p = jnp.exp(s - m_new)
            vt = jnp.concatenate([vt_refs[u][rows(h), :] for u in range(nkv)], axis=1)
            new.append((m_new, alpha * l_prev + jnp.sum(p, axis=0, keepdims=True),
                        alpha * acc_prev + _dot(vt, p.astype(BF16))))
        for h in heads:
            m_ref[h], l_ref[h], acc_ref[h] = new[h]

    @pl.when(j0 == 0)
    def _():
        blkid = lax.broadcasted_iota(jnp.int32, (nb, blk), 0)
        for h in heads:
            m_ref[h] = jnp.full((1, blk), NEG_INF, F32)
            l_ref[h] = jnp.zeros((1, blk), F32)
            acc_ref[h] = jnp.zeros((hd, blk), F32)
            k1, k2, _ = _split3(km_ref[:, rows(h)])
            qt = qt_ref[rows(h), :]
            gate = _dot(k1, qt) + _dot(k2, qt)
            gate = jnp.where(blkid < qi, gate, -jnp.inf)
            cnt = jnp.zeros((nb, blk), F32)
            for i in range(nb):
                gi = gate[i:i + 1, :]
                ahead = (gi > gate) | ((gi == gate) & (blkid > i))
                cnt = cnt + jnp.where(ahead, 1.0, 0.0)
            msk_ref[h, 0:nb, :] = jnp.where(cnt < MOBA_TOPK, 0.0, NEG_INF)
            msk_ref[h, nb:, :] = jnp.full((msk_ref.shape[1] - nb, blk), NEG_INF, F32)

        def bias(h, u):
            if u == 0:
                return bown_ref[h]
            return (bprev_ref[h] if u == 1 else bfar_ref[h]) + mask_row(h, u)

        update(bias)

    @pl.when(j0 > 0)
    def _():
        update(lambda h, u: bfar_ref[h] + mask_row(h, j0 + u))

    @pl.when(j0 + nkv > qi)
    def _():
        for h in heads:
            o_ref[:, rows(h)] = (acc_ref[h] / l_ref[h]).T.astype(o_ref.dtype)


def _rel_bucket(dist):
    n = jnp.maximum(dist, 0)
    max_exact = REL_BUCKETS // 2
    nf = jnp.maximum(n, max_exact).astype(F32)
    large = max_exact + (jnp.log(nf / max_exact) / math.log(REL_MAX_DIST / max_exact)
                         * (REL_BUCKETS - max_exact)).astype(jnp.int32)
    large = jnp.minimum(large, REL_BUCKETS - 1)
    return jnp.where(n < max_exact, n, large)


def _bias_lookup(rel_bias, dist):
    bucket = _rel_bucket(dist)[None]
    out = jnp.zeros((rel_bias.shape[1],) + dist.shape, F32)
    for b in range(REL_BUCKETS):
        out = jnp.where(bucket == b, rel_bias[b].astype(F32).reshape((-1,) + (1,) * dist.ndim), out)
    return out


def _moba(qt, k16, vt, kmean, rel_bias, bsz, seq):
    blk = MOBA_BLOCK
    nb = seq // blk
    w = MOBA_WIDTH
    assert blk >= REL_MAX_DIST
    kk = jnp.arange(blk)[:, None]
    qq = jnp.arange(blk)[None, :]
    b_own = jnp.where((qq >= kk)[None], _bias_lookup(rel_bias, qq - kk), NEG_INF)
    b_prev = _bias_lookup(rel_bias, blk + qq - kk)
    b_far = jnp.broadcast_to(_bias_lookup(rel_bias, jnp.full((1, 1), 2 * blk)), (MOBA_HEADS, 1, blk))
    nkv = MOBA_KV_PER_STEP
    steps = [(qi, j0) for qi in range(nb) for j0 in range(0, qi + 1, nkv)]
    qi_arr = jnp.asarray(np.array([s[0] for s in steps], np.int32))
    j0_arr = jnp.asarray(np.array([s[1] for s in steps], np.int32))

    def kblk(u):
        return lambda b, s, qi, j0: (b * nb + jnp.maximum(qi[s] - j0[s] - u, 0), 0)

    def vblk(u):
        return lambda b, s, qi, j0: (b, 0, jnp.maximum(qi[s] - j0[s] - u, 0))

    grid_spec = pltpu.PrefetchScalarGridSpec(
        num_scalar_prefetch=2,
        grid=(bsz, len(steps)),
        in_specs=(
            [pl.BlockSpec((None, w, blk), lambda b, s, qi, j0: (b, 0, qi[s]))]
            + [pl.BlockSpec((blk, w), kblk(u)) for u in range(nkv)]
            + [pl.BlockSpec((None, w, blk), vblk(u)) for u in range(nkv)]
            + [pl.BlockSpec((None, nb, w), lambda b, s, qi, j0: (b, 0, 0)),
               pl.BlockSpec((MOBA_HEADS, blk, blk), lambda b, s, qi, j0: (0, 0, 0)),
               pl.BlockSpec((MOBA_HEADS, blk, blk), lambda b, s, qi, j0: (0, 0, 0)),
               pl.BlockSpec((MOBA_HEADS, 1, blk), lambda b, s, qi, j0: (0, 0, 0))]),
        out_specs=pl.BlockSpec((blk, w), lambda b, s, qi, j0: (b * nb + qi[s], 0)),
        scratch_shapes=[
            pltpu.VMEM((MOBA_HEADS, 1, blk), F32),
            pltpu.VMEM((MOBA_HEADS, 1, blk), F32),
            pltpu.VMEM((MOBA_HEADS, MOBA_HEAD_DIM, blk), F32),
            pltpu.VMEM((MOBA_HEADS, nb + MASK_PAD_ROWS, blk), F32),
        ],
    )
    return pl.pallas_call(
        _moba_kernel,
        out_shape=jax.ShapeDtypeStruct((bsz * seq, w), BF16),
        grid_spec=grid_spec,
        compiler_params=_cparams(("parallel", "arbitrary")),
        name="moba",
    )(qi_arr, j0_arr, qt, *([k16] * nkv), *([vt] * nkv), kmean, b_own, b_prev, b_far)


def _sconv_kernel(x_ref, bg_ref, cg_ref, w_ref, o_ref, ext_ref):
    seq = x_ref.shape[0]
    u = cg_ref[...] * x_ref[...]
    ext_ref[0:CONV_HALO, :] = jnp.zeros((CONV_HALO, u.shape[1]), F32)
    ext_ref[CONV_HALO:, :] = u
    width = w_ref.shape[0]
    y = u * w_ref[width - 1:width, :]
    for s in range(1, width):
        y = y + ext_ref[CONV_HALO - s:CONV_HALO - s + seq, :] * w_ref[width - 1 - s:width - s, :]
    o_ref[...] = (bg_ref[...] * y).astype(o_ref.dtype)


def _sconv(proj, conv_w, bsz, seq):
    cw = SCONV_WIDTH
    nc = cw // LANES
    c0 = COL_C // LANES
    return pl.pallas_call(
        _sconv_kernel,
        out_shape=jax.ShapeDtypeStruct((bsz * seq, cw), BF16),
        grid=(bsz, nc),
        in_specs=[
            pl.BlockSpec((seq, LANES), lambda b, c: (b, c0 + c)),
            pl.BlockSpec((seq, LANES), lambda b, c: (b, c0 + nc + c)),
            pl.BlockSpec((seq, LANES), lambda b, c: (b, c0 + 2 * nc + c)),
            pl.BlockSpec((SCONV_K, LANES), lambda b, c: (0, c)),
        ],
        out_specs=pl.BlockSpec((seq, LANES), lambda b, c: (b, c)),
        scratch_shapes=[pltpu.VMEM((CONV_HALO + seq, LANES), F32)],
        compiler_params=_cparams(("parallel", "parallel")),
        name="sconv",
    )(proj, proj, proj, conv_w)


def _mixer(x, l, mix_norm, w_in, w_in16, w_bc16, conv_qkv_w, a_log, dt_bias, gdn_norm_w, conv_c_w, rel_bias,
           bsz, seq):
    proj, ba = _proj(x, l, mix_norm, w_in, w_in16, w_bc16)
    o_a = _gdn(proj, ba, conv_qkv_w, a_log, dt_bias, gdn_norm_w, bsz, seq)
    o_b = _moba(*_moba_prep(proj, bsz, seq), rel_bias, bsz, seq)
    o_c = _sconv(proj, conv_c_w, bsz, seq)
    return o_a, o_b, o_c


def kernel(x, ffn1_norm, ffn1_w_gate, ffn1_w_up, ffn1_w_down, mix_norm, w_in, conv_qkv_w, a_log, dt_bias,
           gdn_norm_w, conv_c_w, w_out, ffn2_norm, ffn2_w_gate, ffn2_w_up, ffn2_w_down, rel_bias, final_norm):
    bsz, seq, d = x.shape
    x = x.reshape(bsz * seq, d)
    ffn1 = [w.astype(BF16) for w in (ffn1_w_gate, ffn1_w_up, ffn1_w_down)]
    ffn2 = [w.astype(BF16) for w in (ffn2_w_gate, ffn2_w_up, ffn2_w_down)]
    w_in = jnp.swapaxes(w_in, 1, 2)
    w_in16 = w_in.astype(BF16)
    w_bc16 = w_in16[:, COL_B + 2 * GDN_HEADS:, :]
    w_out16 = w_out.astype(BF16)
    for l in range(DEPTH):
        x = _ffn(x, l, ffn1_norm[l], *ffn1, final_norm, False)
        mixed = _mixer(x, l, mix_norm[l], w_in, w_in16, w_bc16, conv_qkv_w[l], a_log[l], dt_bias[l],
                       gdn_norm_w[l], conv_c_w[l], rel_bias, bsz, seq)
        x = _ffn(x, l, ffn2_norm[l], *ffn2, final_norm, l == DEPTH - 1, mixer_out=(*mixed, w_out16))
    return x.reshape(bsz, seq, d)
```

```python
import functools
import math

import jax
import jax.numpy as jnp
import numpy as np
from jax import lax
from jax.experimental import pallas as pl
from jax.experimental.pallas import tpu as pltpu

F32 = jnp.float32
BF16 = jnp.bfloat16

D_MODEL = 2048
DEPTH = 2
EPS = 1e-6
D_FF = 5632

GDN_HEAD_DIM = 128
GDN_HEADS = 8
GDN_WIDTH = GDN_HEADS * GDN_HEAD_DIM
GDN_CONV = 4
GDN_CHUNK = 64

MOBA_HEAD_DIM = 128
MOBA_HEADS = 4
MOBA_WIDTH = MOBA_HEADS * MOBA_HEAD_DIM
MOBA_BLOCK = 256
MOBA_TOPK = 3

SCONV_WIDTH = D_MODEL - GDN_WIDTH - MOBA_WIDTH
SCONV_K = 3

REL_BUCKETS = 32
REL_MAX_DIST = 128
NEG_INF = -1e30

LANES = 128
CONV_HALO = 8

PROJ_MAIN = 4 * GDN_WIDTH + 3 * MOBA_WIDTH + 3 * SCONV_WIDTH
COL_B = 4 * GDN_WIDTH
COL_C = COL_B + 3 * MOBA_WIDTH

VMEM_LIMIT = 56 * 1024 * 1024

TM_FFN = 512
TF_FFN = 512
TM_PROJ = 1024
TN_PROJ = 1024
TM_OUT = 512
TB_GDN = 256
HEADS_PER_TRIP = 8
MOBA_KV_PER_STEP = 8
MASK_PAD_ROWS = 8


def _cparams(sem):
    return pltpu.CompilerParams(dimension_semantics=sem, vmem_limit_bytes=VMEM_LIMIT)


def _rms(x, w):
    return x * lax.rsqrt(jnp.mean(x * x, axis=-1, keepdims=True) + EPS) * w


def _split3(x):
    x1 = x.astype(BF16)
    r1 = x - x1.astype(F32)
    x2 = r1.astype(BF16)
    x3 = (r1 - x2.astype(F32)).astype(BF16)
    return x1, x2, x3


def _dot(a, b):
    return jnp.dot(a, b, preferred_element_type=F32)


def _dot_nt(a, b):
    return lax.dot_general(a, b, (((1,), (1,)), ((), ())), preferred_element_type=F32)


def _dot_tn(a, b):
    return lax.dot_general(a, b, (((0,), (0,)), ((), ())), preferred_element_type=F32)


def _ffn_kernel(x_ref, nw_ref, wg_ref, wu_ref, wd_ref, fw_ref, o_ref, hn_ref, acc_ref, *, n_f, final_norm):
    f = pl.program_id(1)

    @pl.when(f == 0)
    def _():
        hn_ref[...] = _rms(x_ref[...], nw_ref[...]).astype(BF16)
        acc_ref[...] = jnp.zeros_like(acc_ref)

    hn = hn_ref[...]
    g = _dot(hn, wg_ref[...])
    u = _dot(hn, wu_ref[...])
    a = (g * jax.nn.sigmoid(g) * u).astype(BF16)
    acc_ref[...] += _dot(a, wd_ref[...])

    @pl.when(f == n_f - 1)
    def _():
        y = x_ref[...] + 0.5 * acc_ref[...]
        o_ref[...] = _rms(y, fw_ref[...]) if final_norm else y


def _ffn(x, l, nw, wg, wu, wd, fw, final_norm):
    n, d = x.shape
    ff = wg.shape[2]
    n_f = ff // TF_FFN
    return pl.pallas_call(
        functools.partial(_ffn_kernel, n_f=n_f, final_norm=final_norm),
        out_shape=jax.ShapeDtypeStruct((n, d), F32),
        grid=(n // TM_FFN, n_f),
        in_specs=[
            pl.BlockSpec((TM_FFN, d), lambda i, f: (i, 0)),
            pl.BlockSpec((1, d), lambda i, f: (0, 0)),
            pl.BlockSpec((None, d, TF_FFN), lambda i, f: (l, 0, f)),
            pl.BlockSpec((None, d, TF_FFN), lambda i, f: (l, 0, f)),
            pl.BlockSpec((None, TF_FFN, d), lambda i, f: (l, f, 0)),
            pl.BlockSpec((1, d), lambda i, f: (0, 0)),
        ],
        out_specs=pl.BlockSpec((TM_FFN, d), lambda i, f: (i, 0)),
        scratch_shapes=[pltpu.VMEM((TM_FFN, d), BF16), pltpu.VMEM((TM_FFN, d), F32)],
        compiler_params=_cparams(("parallel", "arbitrary")),
        name="ffn",
    )(x, nw.reshape(1, d), wg, wu, wd, fw.reshape(1, d))


def _proj_kernel(x_ref, nw_ref, wa_ref, wbc_ref, wba_ref, o_ref, ba_ref, hn_ref, *, n_a):
    j = pl.program_id(1)

    @pl.when(j == 0)
    def _():
        h = _rms(x_ref[...], nw_ref[...])
        h1, h2, _ = _split3(h)
        hn_ref[...] = h1
        w = wba_ref[...]
        w1 = w.astype(BF16)
        w2 = (w - w1.astype(F32)).astype(BF16)
        ba_ref[...] = _dot_nt(h1, w1) + (_dot_nt(h1, w2) + _dot_nt(h2, w1))

    @pl.when(j < n_a)
    def _():
        o_ref[...] = _dot_nt(hn_ref[...], wa_ref[...])

    @pl.when(j >= n_a)
    def _():
        o_ref[...] = _dot_nt(hn_ref[...], wbc_ref[...])


def _proj(x, l, nw, w_in, w_in16, w_bc16):
    n, d = x.shape
    n_a = COL_B // TN_PROJ
    ba_blk = COL_B // LANES
    return pl.pallas_call(
        functools.partial(_proj_kernel, n_a=n_a),
        out_shape=(jax.ShapeDtypeStruct((n, PROJ_MAIN), F32), jax.ShapeDtypeStruct((n, LANES), F32)),
        grid=(n // TM_PROJ, PROJ_MAIN // TN_PROJ),
        in_specs=[
            pl.BlockSpec((TM_PROJ, d), lambda i, j: (i, 0)),
            pl.BlockSpec((1, d), lambda i, j: (0, 0)),
            pl.BlockSpec((None, TN_PROJ, d), lambda i, j: (l, jnp.minimum(j, n_a - 1), 0)),
            pl.BlockSpec((None, TN_PROJ, d), lambda i, j: (l, jnp.maximum(j - n_a, 0), 0)),
            pl.BlockSpec((None, LANES, d), lambda i, j: (l, ba_blk, 0)),
        ],
        out_specs=(pl.BlockSpec((TM_PROJ, TN_PROJ), lambda i, j: (i, j)),
                   pl.BlockSpec((TM_PROJ, LANES), lambda i, j: (i, 0))),
        scratch_shapes=[pltpu.VMEM((TM_PROJ, d), BF16)],
        compiler_params=_cparams(("parallel", "arbitrary")),
        name="proj",
    )(x, nw.reshape(1, d), w_in16, w_bc16, w_in)


def _conv_silu(x_ref, hist_ref, ext_ref, w_ref, first):
    tb = x_ref.shape[0]

    @pl.when(first)
    def _():
        hist_ref[...] = jnp.zeros_like(hist_ref)

    x = x_ref[...]
    ext_ref[0:CONV_HALO, :] = hist_ref[...]
    ext_ref[CONV_HALO:, :] = x
    hist_ref[...] = x[tb - CONV_HALO:, :]
    width = w_ref.shape[0]
    y = x * w_ref[width - 1:width, :]
    for s in range(1, width):
        y = y + ext_ref[CONV_HALO - s:CONV_HALO - s + tb, :] * w_ref[width - 1 - s:width - s, :]
    return y * jax.nn.sigmoid(y)


def _gdn_kernel(q_ref, k_ref, v_ref, z_ref, ba_ref, wq_ref, wk_ref, wv_ref, alog_ref, dtb_ref, nw_ref,
                o_ref,
                hq_ref, hk_ref, hv_ref, ext_ref, qc_ref, kc_ref, vc_ref, gc_ref, gct_ref, beta_ref,
                s_ref):
    t = pl.program_id(1)
    tb = q_ref.shape[0]
    nchunk = tb // GDN_CHUNK
    dk = GDN_HEAD_DIM
    first = t == 0

    qc_ref[...] = _conv_silu(q_ref, hq_ref, ext_ref, wq_ref, first)
    kc_ref[...] = _conv_silu(k_ref, hk_ref, ext_ref, wk_ref, first)
    vc_ref[...] = _conv_silu(v_ref, hv_ref, ext_ref, wv_ref, first)

    @pl.when(first)
    def _():
        s_ref[...] = jnp.zeros_like(s_ref)

    ba = ba_ref[...]
    beta_ref[...] = jax.nn.sigmoid(ba)
    xg = ba + dtb_ref[...]
    softplus = jnp.maximum(xg, 0.0) + jnp.log1p(jnp.exp(-jnp.abs(xg)))
    g = -jnp.exp(alog_ref[...]) * softplus
    row = lax.broadcasted_iota(jnp.int32, (tb, tb), 0)
    col = lax.broadcasted_iota(jnp.int32, (tb, tb), 1)
    shift = int(math.log2(GDN_CHUNK))
    same = jnp.right_shift(row, shift) == jnp.right_shift(col, shift)
    incl = same & (col <= row)
    strict = same & (col < row)
    tri = jnp.where(incl, 1.0, 0.0).astype(BF16)
    g1, g2, g3 = _split3(g)
    gc = _dot(tri, g1) + (_dot(tri, g2) + _dot(tri, g3))
    gc_ref[...] = gc
    gct_ref[...] = gc.T

    lane = lax.broadcasted_iota(jnp.int32, (tb, LANES), 1)
    eye = jnp.where(row == col, 1.0, 0.0)

    def head(h, s, done):
        c0 = pl.multiple_of(h * dk, dk)
        q = qc_ref[:, pl.ds(c0, dk)]
        k = kc_ref[:, pl.ds(c0, dk)]
        v = vc_ref[:, pl.ds(c0, dk)]
        z = z_ref[:, pl.ds(c0, dk)]
        q = q * lax.rsqrt(jnp.sum(q * q, axis=-1, keepdims=True) + EPS) * (dk ** -0.5)
        k = k * lax.rsqrt(jnp.sum(k * k, axis=-1, keepdims=True) + EPS)
        gcol = jnp.sum(jnp.where(lane == GDN_HEADS + h, gc_ref[...], 0.0), axis=-1, keepdims=True)
        bcol = jnp.sum(jnp.where(lane == h, beta_ref[...], 0.0), axis=-1, keepdims=True)
        grow = gct_ref[pl.ds(GDN_HEADS + h, 1), :]

        decay = jnp.where(incl, jnp.exp(jnp.where(incl, gcol - grow, 0.0)), 0.0)
        kb = k * bcol
        k16 = k.astype(BF16)
        kk = _dot_nt(kb.astype(BF16), k16)
        qk = _dot_nt(q.astype(BF16), k16)
        yield
        a_intra = (qk * decay).astype(BF16)
        m = -jnp.where(strict, kk * decay, 0.0)
        p = eye + m
        for _ in range(int(math.log2(GDN_CHUNK)) - 1):
            m16 = m.astype(BF16)
            m = _dot(m16, m16)
            yield
            p = p + _dot(p.astype(BF16), m.astype(BF16))
        eg = jnp.exp(gcol)
        rhs = jnp.concatenate([v * bcol, kb * eg], axis=1).astype(BF16)
        yield
        uw = _dot(p.astype(BF16), rhs)
        yield
        u = uw[:, :dk]
        w16 = uw[:, dk:].astype(BF16)
        qg16 = (q * eg).astype(BF16)

        zero16 = jnp.zeros((GDN_CHUNK, dk), BF16)
        outs = []
        for c in range(nchunk):
            r0, r1 = c * GDN_CHUNK, (c + 1) * GDN_CHUNK
            s16 = s.astype(BF16)
            ws = _dot(w16[r0:r1], s16)
            qs = _dot(qg16[r0:r1], s16)
            yield
            v_new = (u[r0:r1] - ws).astype(BF16)
            vn_full = jnp.concatenate([zero16] * c + [v_new] + [zero16] * (nchunk - 1 - c), axis=0)
            gl = gcol[r1 - 1:r1, :]
            kg = k[r0:r1] * jnp.exp(gl - gcol[r0:r1])
            s = s * jnp.exp(gl) + _dot_tn(kg.astype(BF16), v_new)
            outs.append(qs + _dot(a_intra[r0:r1], vn_full))
            yield
        o = jnp.concatenate(outs, axis=0)
        o = _rms(o, nw_ref[...]) * (z * jax.nn.sigmoid(z))
        done.append((o.astype(o_ref.dtype), s))

    def head_group(i, carry):
        hs = [i * HEADS_PER_TRIP + u for u in range(HEADS_PER_TRIP)]
        results = []
        live = [head(h, s_ref[h], results) for h in hs]
        while live:
            live = [g for g in live if next(g, live) is not live]
        for h, (o, s) in zip(hs, results):
            s_ref[h] = s
            o_ref[:, pl.ds(pl.multiple_of(h * dk, dk), dk)] = o
        return carry

    lax.fori_loop(0, GDN_HEADS // HEADS_PER_TRIP, head_group, 0)


def _gdn(proj, ba, conv_w, a_log, dt_bias, norm_w, bsz, seq):
    tb = TB_GDN
    nt = seq // tb
    gw = GDN_WIDTH
    alog = jnp.zeros((1, LANES), F32).at[0, GDN_HEADS:2 * GDN_HEADS].set(a_log)
    dtb = jnp.zeros((1, LANES), F32).at[0, GDN_HEADS:2 * GDN_HEADS].set(dt_bias)

    def rows(b, t):
        return b * nt + t

    return pl.pallas_call(
        _gdn_kernel,
        out_shape=jax.ShapeDtypeStruct((bsz * seq, gw), BF16),
        grid=(bsz, nt),
        in_specs=[
            pl.BlockSpec((tb, gw), lambda b, t: (rows(b, t), 0)),
            pl.BlockSpec((tb, gw), lambda b, t: (rows(b, t), 1)),
            pl.BlockSpec((tb, gw), lambda b, t: (rows(b, t), 2)),
            pl.BlockSpec((tb, gw), lambda b, t: (rows(b, t), 3)),
            pl.BlockSpec((tb, LANES), lambda b, t: (rows(b, t), 0)),
            pl.BlockSpec((GDN_CONV, gw), lambda b, t: (0, 0)),
            pl.BlockSpec((GDN_CONV, gw), lambda b, t: (0, 1)),
            pl.BlockSpec((GDN_CONV, gw), lambda b, t: (0, 2)),
            pl.BlockSpec((1, LANES), lambda b, t: (0, 0)),
            pl.BlockSpec((1, LANES), lambda b, t: (0, 0)),
            pl.BlockSpec((1, GDN_HEAD_DIM), lambda b, t: (0, 0)),
        ],
        out_specs=pl.BlockSpec((tb, gw), lambda b, t: (rows(b, t), 0)),
        scratch_shapes=[
            pltpu.VMEM((CONV_HALO, gw), F32), pltpu.VMEM((CONV_HALO, gw), F32), pltpu.VMEM((CONV_HALO, gw), F32),
            pltpu.VMEM((CONV_HALO + tb, gw), F32),
            pltpu.VMEM((tb, gw), F32), pltpu.VMEM((tb, gw), F32), pltpu.VMEM((tb, gw), F32),
            pltpu.VMEM((tb, LANES), F32), pltpu.VMEM((LANES, tb), F32), pltpu.VMEM((tb, LANES), F32),
            pltpu.VMEM((GDN_HEADS, GDN_HEAD_DIM, GDN_HEAD_DIM), F32),
        ],
        compiler_params=_cparams(("parallel", "arbitrary")),
        name="gdn",
    )(proj, proj, proj, proj, ba, conv_w, conv_w, conv_w, alog, dtb, norm_w.reshape(1, GDN_HEAD_DIM))


def _moba_prep_kernel(q_ref, k_ref, v_ref, qt_ref, k16_ref, vt_ref, km_ref):
    t = pl.program_id(1)
    qt_ref[...] = (q_ref[...] * (MOBA_HEAD_DIM ** -0.5)).T.astype(BF16)
    vt_ref[...] = v_ref[...].T.astype(BF16)
    k = k_ref[...]
    k16_ref[...] = k.astype(BF16)
    km_ref[pl.ds(t, 1), :] = jnp.mean(k, axis=0, keepdims=True)


def _moba_prep(proj, bsz, seq):
    blk = MOBA_BLOCK
    nb = seq // blk
    w = MOBA_WIDTH
    cb = COL_B // w
    return pl.pallas_call(
        _moba_prep_kernel,
        out_shape=(jax.ShapeDtypeStruct((bsz, w, seq), BF16),
                   jax.ShapeDtypeStruct((bsz * seq, w), BF16),
                   jax.ShapeDtypeStruct((bsz, w, seq), BF16),
                   jax.ShapeDtypeStruct((bsz, nb, w), F32)),
        grid=(bsz, nb),
        in_specs=[pl.BlockSpec((blk, w), lambda b, t: (b * nb + t, cb)),
                  pl.BlockSpec((blk, w), lambda b, t: (b * nb + t, cb + 1)),
                  pl.BlockSpec((blk, w), lambda b, t: (b * nb + t, cb + 2))],
        out_specs=(pl.BlockSpec((None, w, blk), lambda b, t: (b, 0, t)),
                   pl.BlockSpec((blk, w), lambda b, t: (b * nb + t, 0)),
                   pl.BlockSpec((None, w, blk), lambda b, t: (b, 0, t)),
                   pl.BlockSpec((None, nb, w), lambda b, t: (b, 0, 0))),
        compiler_params=_cparams(("parallel", "arbitrary")),
        name="moba_prep",
    )(proj, proj, proj)


def _moba_kernel(qi_ref, j0_ref, *refs):
    nkv = MOBA_KV_PER_STEP
    qt_ref = refs[0]
    k_refs = refs[1:1 + nkv]
    vt_refs = refs[1 + nkv:1 + 2 * nkv]
    km_ref, bown_ref, bprev_ref, bfar_ref, o_ref, m_ref, l_ref, acc_ref, msk_ref = refs[1 + 2 * nkv:]
    step = pl.program_id(1)
    qi = qi_ref[step]
    j0 = j0_ref[step]
    blk = MOBA_BLOCK
    hd = MOBA_HEAD_DIM
    nb = km_ref.shape[0]
    heads = range(MOBA_HEADS)

    def rows(h):
        return slice(h * hd, (h + 1) * hd)

    def mask_row(h, j):
        return msk_ref[h, pl.ds(jnp.where(j <= qi, qi - j, nb), 1), :]

    def update(bias):
        old = [(m_ref[h], l_ref[h], acc_ref[h]) for h in heads]
        new = []
        for h in heads:
            m_prev, l_prev, acc_prev = old[h]
            qt = qt_ref[rows(h), :]
            s = jnp.concatenate([_dot(k_refs[u][:, rows(h)], qt) + bias(h, u) for u in range(nkv)], axis=0)
            m_new = jnp.maximum(m_prev, jnp.max(s, axis=0, keepdims=True))
            alpha = jnp.exp(m_prev - m_new)
            p = jnp.exp(s - m_new)
            vt = jnp.concatenate([vt_refs[u][rows(h), :] for u in range(nkv)], axis=1)
            new.append((m_new, alpha * l_prev + jnp.sum(p, axis=0, keepdims=True),
                        alpha * acc_prev + _dot(vt, p.astype(BF16))))
        for h in heads:
            m_ref[h], l_ref[h], acc_ref[h] = new[h]

    @pl.when(j0 == 0)
    def _():
        blkid = lax.broadcasted_iota(jnp.int32, (nb, blk), 0)
        for h in heads:
            m_ref[h] = jnp.full((1, blk), NEG_INF, F32)
            l_ref[h] = jnp.zeros((1, blk), F32)
            acc_ref[h] = jnp.zeros((hd, blk), F32)
            k1, k2, _ = _split3(km_ref[:, rows(h)])
            qt = qt_ref[rows(h), :]
            gate = _dot(k1, qt) + _dot(k2, qt)
            gate = jnp.where(blkid < qi, gate, -jnp.inf)
            cnt = jnp.zeros((nb, blk), F32)
            for i in range(nb):
                gi = gate[i:i + 1, :]
                ahead = (gi > gate) | ((gi == gate) & (blkid > i))
                cnt = cnt + jnp.where(ahead, 1.0, 0.0)
            msk_ref[h, 0:nb, :] = jnp.where(cnt < MOBA_TOPK, 0.0, NEG_INF)
            msk_ref[h, nb:, :] = jnp.full((msk_ref.shape[1] - nb, blk), NEG_INF, F32)

        def bias(h, u):
            if u == 0:
                return bown_ref[h]
            return (bprev_ref[h] if u == 1 else bfar_ref[h]) + mask_row(h, u)

        update(bias)

    @pl.when(j0 > 0)
    def _():
        update(lambda h, u: bfar_ref[h] + mask_row(h, j0 + u))

    @pl.when(j0 + nkv > qi)
    def _():
        for h in heads:
            o_ref[:, rows(h)] = (acc_ref[h] / l_ref[h]).T.astype(o_ref.dtype)


def _rel_bucket(dist):
    n = jnp.maximum(dist, 0)
    max_exact = REL_BUCKETS // 2
    nf = jnp.maximum(n, max_exact).astype(F32)
    large = max_exact + (jnp.log(nf / max_exact) / math.log(REL_MAX_DIST / max_exact)
                         * (REL_BUCKETS - max_exact)).astype(jnp.int32)
    large = jnp.minimum(large, REL_BUCKETS - 1)
    return jnp.where(n < max_exact, n, large)


def _bias_lookup(rel_bias, dist):
    bucket = _rel_bucket(dist)[None]
    out = jnp.zeros((rel_bias.shape[1],) + dist.shape, F32)
    for b in range(REL_BUCKETS):
        out = jnp.where(bucket == b, rel_bias[b].astype(F32).reshape((-1,) + (1,) * dist.ndim), out)
    return out


def _moba(qt, k16, vt, kmean, rel_bias, bsz, seq):
    blk = MOBA_BLOCK
    nb = seq // blk
    w = MOBA_WIDTH
    assert blk >= REL_MAX_DIST
    kk = jnp.arange(blk)[:, None]
    qq = jnp.arange(blk)[None, :]
    b_own = jnp.where((qq >= kk)[None], _bias_lookup(rel_bias, qq - kk), NEG_INF)
    b_prev = _bias_lookup(rel_bias, blk + qq - kk)
    b_far = jnp.broadcast_to(_bias_lookup(rel_bias, jnp.full((1, 1), 2 * blk)), (MOBA_HEADS, 1, blk))
    nkv = MOBA_KV_PER_STEP
    steps = [(qi, j0) for qi in range(nb) for j0 in range(0, qi + 1, nkv)]
    qi_arr = jnp.asarray(np.array([s[0] for s in steps], np.int32))
    j0_arr = jnp.asarray(np.array([s[1] for s in steps], np.int32))

    def kblk(u):
        return lambda b, s, qi, j0: (b * nb + jnp.maximum(qi[s] - j0[s] - u, 0), 0)

    def vblk(u):
        return lambda b, s, qi, j0: (b, 0, jnp.maximum(qi[s] - j0[s] - u, 0))

    grid_spec = pltpu.PrefetchScalarGridSpec(
        num_scalar_prefetch=2,
        grid=(bsz, len(steps)),
        in_specs=(
            [pl.BlockSpec((None, w, blk), lambda b, s, qi, j0: (b, 0, qi[s]))]
            + [pl.BlockSpec((blk, w), kblk(u)) for u in range(nkv)]
            + [pl.BlockSpec((None, w, blk), vblk(u)) for u in range(nkv)]
            + [pl.BlockSpec((None, nb, w), lambda b, s, qi, j0: (b, 0, 0)),
               pl.BlockSpec((MOBA_HEADS, blk, blk), lambda b, s, qi, j0: (0, 0, 0)),
               pl.BlockSpec((MOBA_HEADS, blk, blk), lambda b, s, qi, j0: (0, 0, 0)),
               pl.BlockSpec((MOBA_HEADS, 1, blk), lambda b, s, qi, j0: (0, 0, 0))]),
        out_specs=pl.BlockSpec((blk, w), lambda b, s, qi, j0: (b * nb + qi[s], 0)),
        scratch_shapes=[
            pltpu.VMEM((MOBA_HEADS, 1, blk), F32),
            pltpu.VMEM((MOBA_HEADS, 1, blk), F32),
            pltpu.VMEM((MOBA_HEADS, MOBA_HEAD_DIM, blk), F32),
            pltpu.VMEM((MOBA_HEADS, nb + MASK_PAD_ROWS, blk), F32),
        ],
    )
    return pl.pallas_call(
        _moba_kernel,
        out_shape=jax.ShapeDtypeStruct((bsz * seq, w), BF16),
        grid_spec=grid_spec,
        compiler_params=_cparams(("parallel", "arbitrary")),
        name="moba",
    )(qi_arr, j0_arr, qt, *([k16] * nkv), *([vt] * nkv), kmean, b_own, b_prev, b_far)


def _sconv_kernel(x_ref, bg_ref, cg_ref, w_ref, o_ref, ext_ref):
    seq = x_ref.shape[0]
    u = cg_ref[...] * x_ref[...]
    ext_ref[0:CONV_HALO, :] = jnp.zeros((CONV_HALO, u.shape[1]), F32)
    ext_ref[CONV_HALO:, :] = u
    width = w_ref.shape[0]
    y = u * w_ref[width - 1:width, :]
    for s in range(1, width):
        y = y + ext_ref[CONV_HALO - s:CONV_HALO - s + seq, :] * w_ref[width - 1 - s:width - s, :]
    o_ref[...] = (bg_ref[...] * y).astype(o_ref.dtype)


def _sconv(proj, conv_w, bsz, seq):
    cw = SCONV_WIDTH
    nc = cw // LANES
    c0 = COL_C // LANES
    return pl.pallas_call(
        _sconv_kernel,
        out_shape=jax.ShapeDtypeStruct((bsz * seq, cw), BF16),
        grid=(bsz, nc),
        in_specs=[
            pl.BlockSpec((seq, LANES), lambda b, c: (b, c0 + c)),
            pl.BlockSpec((seq, LANES), lambda b, c: (b, c0 + nc + c)),
            pl.BlockSpec((seq, LANES), lambda b, c: (b, c0 + 2 * nc + c)),
            pl.BlockSpec((SCONV_K, LANES), lambda b, c: (0, c)),
        ],
        out_specs=pl.BlockSpec((seq, LANES), lambda b, c: (b, c)),
        scratch_shapes=[pltpu.VMEM((CONV_HALO + seq, LANES), F32)],
        compiler_params=_cparams(("parallel", "parallel")),
        name="sconv",
    )(proj, proj, proj, conv_w)


def _out_kernel(x_ref, oa_ref, ob_ref, oc_ref, w_ref, o_ref):
    a0, a1, a2 = GDN_WIDTH, GDN_WIDTH + MOBA_WIDTH, D_MODEL
    y = _dot(oa_ref[...], w_ref[0:a0, :])
    y = y + _dot(ob_ref[...], w_ref[a0:a1, :])
    y = y + _dot(oc_ref[...], w_ref[a1:a2, :])
    o_ref[...] = x_ref[...] + y


def _out(x, l, oa, ob, oc, w):
    n, d = x.shape
    return pl.pallas_call(
        _out_kernel,
        out_shape=jax.ShapeDtypeStruct((n, d), F32),
        grid=(n // TM_OUT,),
        in_specs=[
            pl.BlockSpec((TM_OUT, d), lambda i: (i, 0)),
            pl.BlockSpec((TM_OUT, GDN_WIDTH), lambda i: (i, 0)),
            pl.BlockSpec((TM_OUT, MOBA_WIDTH), lambda i: (i, 0)),
            pl.BlockSpec((TM_OUT, SCONV_WIDTH), lambda i: (i, 0)),
            pl.BlockSpec((None, d, d), lambda i: (l, 0, 0)),
        ],
        out_specs=pl.BlockSpec((TM_OUT, d), lambda i: (i, 0)),
        compiler_params=_cparams(("parallel",)),
        name="out_proj",
    )(x, oa, ob, oc, w)


def _mixer(x, l, mix_norm, w_in, w_in16, w_bc16, conv_qkv_w, a_log, dt_bias, gdn_norm_w, conv_c_w, w_out, rel_bias,
           bsz, seq):
    proj, ba = _proj(x, l, mix_norm, w_in, w_in16, w_bc16)
    o_a = _gdn(proj, ba, conv_qkv_w, a_log, dt_bias, gdn_norm_w, bsz, seq)
    o_b = _moba(*_moba_prep(proj, bsz, seq), rel_bias, bsz, seq)
    o_c = _sconv(proj, conv_c_w, bsz, seq)
    return _out(x, l, o_a, o_b, o_c, w_out)


def kernel(x, ffn1_norm, ffn1_w_gate, ffn1_w_up, ffn1_w_down, mix_norm, w_in, conv_qkv_w, a_log, dt_bias,
           gdn_norm_w, conv_c_w, w_out, ffn2_norm, ffn2_w_gate, ffn2_w_up, ffn2_w_down, rel_bias, final_norm):
    bsz, seq, d = x.shape
    x = x.reshape(bsz * seq, d)
    ffn1 = [w.astype(BF16) for w in (ffn1_w_gate, ffn1_w_up, ffn1_w_down)]
    ffn2 = [w.astype(BF16) for w in (ffn2_w_gate, ffn2_w_up, ffn2_w_down)]
    w_in = jnp.swapaxes(w_in, 1, 2)
    w_in16 = w_in.astype(BF16)
    w_bc16 = w_in16[:, COL_B + 2 * GDN_HEADS:, :]
    w_out16 = w_out.astype(BF16)
    for l in range(DEPTH):
        x = _ffn(x, l, ffn1_norm[l], *ffn1, final_norm, False)
        x = _mixer(x, l, mix_norm[l], w_in, w_in16, w_bc16, conv_qkv_w[l], a_log[l], dt_bias[l], gdn_norm_w[l],
                   conv_c_w[l], w_out16, rel_bias, bsz, seq)
        x = _ffn(x, l, ffn2_norm[l], *ffn2, final_norm, l == DEPTH - 1)
    return x.reshape(bsz, seq, d)
```

```python
import functools
import math

import jax
import jax.numpy as jnp
import numpy as np
from jax import lax
from jax.experimental import pallas as pl
from jax.experimental.pallas import tpu as pltpu

F32 = jnp.float32
BF16 = jnp.bfloat16

D_MODEL = 2048
DEPTH = 2
EPS = 1e-6
D_FF = 5632

GDN_HEAD_DIM = 128
GDN_HEADS = 8
GDN_WIDTH = GDN_HEADS * GDN_HEAD_DIM
GDN_CONV = 4
GDN_CHUNK = 64

MOBA_HEAD_DIM = 128
MOBA_HEADS = 4
MOBA_WIDTH = MOBA_HEADS * MOBA_HEAD_DIM
MOBA_BLOCK = 256
MOBA_TOPK = 3

SCONV_WIDTH = D_MODEL - GDN_WIDTH - MOBA_WIDTH
SCONV_K = 3

REL_BUCKETS = 32
REL_MAX_DIST = 128
NEG_INF = -1e30

LANES = 128
CONV_HALO = 8

PROJ_MAIN = 4 * GDN_WIDTH + 3 * MOBA_WIDTH + 3 * SCONV_WIDTH
COL_B = 4 * GDN_WIDTH
COL_C = COL_B + 3 * MOBA_WIDTH

VMEM_LIMIT = 56 * 1024 * 1024

TM_FFN = 1024
TF_FFN = 512
TM_PROJ = 1024
TN_PROJ = 1024
TM_OUT = 512
TB_GDN = 256
HEADS_PER_TRIP = 8
MOBA_KV_PER_STEP = 8
MASK_PAD_ROWS = 8


def _cparams(sem):
    return pltpu.CompilerParams(dimension_semantics=sem, vmem_limit_bytes=VMEM_LIMIT)


def _rms(x, w):
    return x * lax.rsqrt(jnp.mean(x * x, axis=-1, keepdims=True) + EPS) * w


def _split3(x):
    x1 = x.astype(BF16)
    r1 = x - x1.astype(F32)
    x2 = r1.astype(BF16)
    x3 = (r1 - x2.astype(F32)).astype(BF16)
    return x1, x2, x3


def _dot(a, b):
    return jnp.dot(a, b, preferred_element_type=F32)


def _dot_nt(a, b):
    return lax.dot_general(a, b, (((1,), (1,)), ((), ())), preferred_element_type=F32)


def _dot_tn(a, b):
    return lax.dot_general(a, b, (((0,), (0,)), ((), ())), preferred_element_type=F32)


def _ffn_kernel(x_ref, nw_ref, wg_ref, wu_ref, wd_ref, fw_ref, o_ref, hn_ref, *, n_f, final_norm):
    f = pl.program_id(1)

    @pl.when(f == 0)
    def _():
        hn_ref[...] = _rms(x_ref[...], nw_ref[...]).astype(BF16)
        o_ref[...] = jnp.zeros_like(o_ref)

    hn = hn_ref[...]
    g = _dot(hn, wg_ref[...])
    u = _dot(hn, wu_ref[...])
    a = (g * jax.nn.sigmoid(g) * u).astype(BF16)
    o_ref[...] += _dot(a, wd_ref[...])

    @pl.when(f == n_f - 1)
    def _():
        y = x_ref[...] + 0.5 * o_ref[...]
        o_ref[...] = _rms(y, fw_ref[...]) if final_norm else y


def _ffn(x, l, nw, wg, wu, wd, fw, final_norm):
    n, d = x.shape
    ff = wg.shape[2]
    n_f = ff // TF_FFN
    return pl.pallas_call(
        functools.partial(_ffn_kernel, n_f=n_f, final_norm=final_norm),
        out_shape=jax.ShapeDtypeStruct((n, d), F32),
        grid=(n // TM_FFN, n_f),
        in_specs=[
            pl.BlockSpec((TM_FFN, d), lambda i, f: (i, 0), pipeline_mode=pl.Buffered(1)),
            pl.BlockSpec((1, d), lambda i, f: (0, 0)),
            pl.BlockSpec((None, d, TF_FFN), lambda i, f: (l, 0, f)),
            pl.BlockSpec((None, d, TF_FFN), lambda i, f: (l, 0, f)),
            pl.BlockSpec((None, TF_FFN, d), lambda i, f: (l, f, 0)),
            pl.BlockSpec((1, d), lambda i, f: (0, 0)),
        ],
        out_specs=pl.BlockSpec((TM_FFN, d), lambda i, f: (i, 0)),
        scratch_shapes=[pltpu.VMEM((TM_FFN, d), BF16)],
        compiler_params=_cparams(("parallel", "arbitrary")),
        name="ffn",
    )(x, nw.reshape(1, d), wg, wu, wd, fw.reshape(1, d))


def _proj_kernel(x_ref, nw_ref, wa_ref, wbc_ref, wba_ref, o_ref, ba_ref, hn_ref, *, n_a):
    j = pl.program_id(1)

    @pl.when(j == 0)
    def _():
        h = _rms(x_ref[...], nw_ref[...])
        h1, h2, _ = _split3(h)
        hn_ref[...] = h1
        w = wba_ref[...]
        w1 = w.astype(BF16)
        w2 = (w - w1.astype(F32)).astype(BF16)
        ba_ref[...] = _dot_nt(h1, w1) + (_dot_nt(h1, w2) + _dot_nt(h2, w1))

    @pl.when(j < n_a)
    def _():
        o_ref[...] = _dot_nt(hn_ref[...], wa_ref[...])

    @pl.when(j >= n_a)
    def _():
        o_ref[...] = _dot_nt(hn_ref[...], wbc_ref[...])


def _proj(x, l, nw, w_in, w_in16, w_bc16):
    n, d = x.shape
    n_a = COL_B // TN_PROJ
    ba_blk = COL_B // LANES
    return pl.pallas_call(
        functools.partial(_proj_kernel, n_a=n_a),
        out_shape=(jax.ShapeDtypeStruct((n, PROJ_MAIN), F32), jax.ShapeDtypeStruct((n, LANES), F32)),
        grid=(n // TM_PROJ, PROJ_MAIN // TN_PROJ),
        in_specs=[
            pl.BlockSpec((TM_PROJ, d), lambda i, j: (i, 0)),
            pl.BlockSpec((1, d), lambda i, j: (0, 0)),
            pl.BlockSpec((None, TN_PROJ, d), lambda i, j: (l, jnp.minimum(j, n_a - 1), 0)),
            pl.BlockSpec((None, TN_PROJ, d), lambda i, j: (l, jnp.maximum(j - n_a, 0), 0)),
            pl.BlockSpec((None, LANES, d), lambda i, j: (l, ba_blk, 0)),
        ],
        out_specs=(pl.BlockSpec((TM_PROJ, TN_PROJ), lambda i, j: (i, j)),
                   pl.BlockSpec((TM_PROJ, LANES), lambda i, j: (i, 0))),
        scratch_shapes=[pltpu.VMEM((TM_PROJ, d), BF16)],
        compiler_params=_cparams(("parallel", "arbitrary")),
        name="proj",
    )(x, nw.reshape(1, d), w_in16, w_bc16, w_in)


def _conv_silu(x_ref, hist_ref, ext_ref, w_ref, first):
    tb = x_ref.shape[0]

    @pl.when(first)
    def _():
        hist_ref[...] = jnp.zeros_like(hist_ref)

    x = x_ref[...]
    ext_ref[0:CONV_HALO, :] = hist_ref[...]
    ext_ref[CONV_HALO:, :] = x
    hist_ref[...] = x[tb - CONV_HALO:, :]
    width = w_ref.shape[0]
    y = x * w_ref[width - 1:width, :]
    for s in range(1, width):
        y = y + ext_ref[CONV_HALO - s:CONV_HALO - s + tb, :] * w_ref[width - 1 - s:width - s, :]
    return y * jax.nn.sigmoid(y)


def _gdn_kernel(q_ref, k_ref, v_ref, z_ref, ba_ref, wq_ref, wk_ref, wv_ref, alog_ref, dtb_ref, nw_ref,
                o_ref,
                hq_ref, hk_ref, hv_ref, ext_ref, qc_ref, kc_ref, vc_ref, gc_ref, gct_ref, beta_ref,
                s_ref):
    t = pl.program_id(1)
    tb = q_ref.shape[0]
    nchunk = tb // GDN_CHUNK
    dk = GDN_HEAD_DIM
    first = t == 0

    qc_ref[...] = _conv_silu(q_ref, hq_ref, ext_ref, wq_ref, first)
    kc_ref[...] = _conv_silu(k_ref, hk_ref, ext_ref, wk_ref, first)
    vc_ref[...] = _conv_silu(v_ref, hv_ref, ext_ref, wv_ref, first)

    @pl.when(first)
    def _():
        s_ref[...] = jnp.zeros_like(s_ref)

    ba = ba_ref[...]
    beta_ref[...] = jax.nn.sigmoid(ba)
    xg = ba + dtb_ref[...]
    softplus = jnp.maximum(xg, 0.0) + jnp.log1p(jnp.exp(-jnp.abs(xg)))
    g = -jnp.exp(alog_ref[...]) * softplus
    row = lax.broadcasted_iota(jnp.int32, (tb, tb), 0)
    col = lax.broadcasted_iota(jnp.int32, (tb, tb), 1)
    shift = int(math.log2(GDN_CHUNK))
    same = jnp.right_shift(row, shift) == jnp.right_shift(col, shift)
    incl = same & (col <= row)
    strict = same & (col < row)
    tri = jnp.where(incl, 1.0, 0.0).astype(BF16)
    g1, g2, g3 = _split3(g)
    gc = _dot(tri, g1) + (_dot(tri, g2) + _dot(tri, g3))
    gc_ref[...] = gc
    gct_ref[...] = gc.T

    lane = lax.broadcasted_iota(jnp.int32, (tb, LANES), 1)
    eye = jnp.where(row == col, 1.0, 0.0)

    def head(h, s, done):
        c0 = pl.multiple_of(h * dk, dk)
        q = qc_ref[:, pl.ds(c0, dk)]
        k = kc_ref[:, pl.ds(c0, dk)]
        v = vc_ref[:, pl.ds(c0, dk)]
        z = z_ref[:, pl.ds(c0, dk)]
        q = q * lax.rsqrt(jnp.sum(q * q, axis=-1, keepdims=True) + EPS) * (dk ** -0.5)
        k = k * lax.rsqrt(jnp.sum(k * k, axis=-1, keepdims=True) + EPS)
        gcol = jnp.sum(jnp.where(lane == GDN_HEADS + h, gc_ref[...], 0.0), axis=-1, keepdims=True)
        bcol = jnp.sum(jnp.where(lane == h, beta_ref[...], 0.0), axis=-1, keepdims=True)
        grow = gct_ref[pl.ds(GDN_HEADS + h, 1), :]

        decay = jnp.where(incl, jnp.exp(jnp.where(incl, gcol - grow, 0.0)), 0.0)
        kb = k * bcol
        k16 = k.astype(BF16)
        kk = _dot_nt(kb.astype(BF16), k16)
        qk = _dot_nt(q.astype(BF16), k16)
        yield
        a_intra = (qk * decay).astype(BF16)
        m = -jnp.where(strict, kk * decay, 0.0)
        p = eye + m
        for _ in range(int(math.log2(GDN_CHUNK)) - 1):
            m16 = m.astype(BF16)
            m = _dot(m16, m16)
            yield
            p = p + _dot(p.astype(BF16), m.astype(BF16))
        eg = jnp.exp(gcol)
        rhs = jnp.concatenate([v * bcol, kb * eg], axis=1).astype(BF16)
        yield
        uw = _dot(p.astype(BF16), rhs)
        yield
        u = uw[:, :dk]
        w16 = uw[:, dk:].astype(BF16)
        qg16 = (q * eg).astype(BF16)

        zero16 = jnp.zeros((GDN_CHUNK, dk), BF16)
        outs = []
        for c in range(nchunk):
            r0, r1 = c * GDN_CHUNK, (c + 1) * GDN_CHUNK
            s16 = s.astype(BF16)
            ws = _dot(w16[r0:r1], s16)
            qs = _dot(qg16[r0:r1], s16)
            yield
            v_new = (u[r0:r1] - ws).astype(BF16)
            vn_full = jnp.concatenate([zero16] * c + [v_new] + [zero16] * (nchunk - 1 - c), axis=0)
            gl = gcol[r1 - 1:r1, :]
            kg = k[r0:r1] * jnp.exp(gl - gcol[r0:r1])
            s = s * jnp.exp(gl) + _dot_tn(kg.astype(BF16), v_new)
            outs.append(qs + _dot(a_intra[r0:r1], vn_full))
            yield
        o = jnp.concatenate(outs, axis=0)
        o = _rms(o, nw_ref[...]) * (z * jax.nn.sigmoid(z))
        done.append((o.astype(o_ref.dtype), s))

    def head_group(i, carry):
        hs = [i * HEADS_PER_TRIP + u for u in range(HEADS_PER_TRIP)]
        results = []
        live = [head(h, s_ref[h], results) for h in hs]
        while live:
            live = [g for g in live if next(g, live) is not live]
        for h, (o, s) in zip(hs, results):
            s_ref[h] = s
            o_ref[:, pl.ds(pl.multiple_of(h * dk, dk), dk)] = o
        return carry

    lax.fori_loop(0, GDN_HEADS // HEADS_PER_TRIP, head_group, 0)


def _gdn(proj, ba, conv_w, a_log, dt_bias, norm_w, bsz, seq):
    tb = TB_GDN
    nt = seq // tb
    gw = GDN_WIDTH
    alog = jnp.zeros((1, LANES), F32).at[0, GDN_HEADS:2 * GDN_HEADS].set(a_log)
    dtb = jnp.zeros((1, LANES), F32).at[0, GDN_HEADS:2 * GDN_HEADS].set(dt_bias)

    def rows(b, t):
        return b * nt + t

    return pl.pallas_call(
        _gdn_kernel,
        out_shape=jax.ShapeDtypeStruct((bsz * seq, gw), BF16),
        grid=(bsz, nt),
        in_specs=[
            pl.BlockSpec((tb, gw), lambda b, t: (rows(b, t), 0)),
            pl.BlockSpec((tb, gw), lambda b, t: (rows(b, t), 1)),
            pl.BlockSpec((tb, gw), lambda b, t: (rows(b, t), 2)),
            pl.BlockSpec((tb, gw), lambda b, t: (rows(b, t), 3)),
            pl.BlockSpec((tb, LANES), lambda b, t: (rows(b, t), 0)),
            pl.BlockSpec((GDN_CONV, gw), lambda b, t: (0, 0)),
            pl.BlockSpec((GDN_CONV, gw), lambda b, t: (0, 1)),
            pl.BlockSpec((GDN_CONV, gw), lambda b, t: (0, 2)),
            pl.BlockSpec((1, LANES), lambda b, t: (0, 0)),
            pl.BlockSpec((1, LANES), lambda b, t: (0, 0)),
            pl.BlockSpec((1, GDN_HEAD_DIM), lambda b, t: (0, 0)),
        ],
        out_specs=pl.BlockSpec((tb, gw), lambda b, t: (rows(b, t), 0)),
        scratch_shapes=[
            pltpu.VMEM((CONV_HALO, gw), F32), pltpu.VMEM((CONV_HALO, gw), F32), pltpu.VMEM((CONV_HALO, gw), F32),
            pltpu.VMEM((CONV_HALO + tb, gw), F32),
            pltpu.VMEM((tb, gw), F32), pltpu.VMEM((tb, gw), F32), pltpu.VMEM((tb, gw), F32),
            pltpu.VMEM((tb, LANES), F32), pltpu.VMEM((LANES, tb), F32), pltpu.VMEM((tb, LANES), F32),
            pltpu.VMEM((GDN_HEADS, GDN_HEAD_DIM, GDN_HEAD_DIM), F32),
        ],
        compiler_params=_cparams(("parallel", "arbitrary")),
        name="gdn",
    )(proj, proj, proj, proj, ba, conv_w, conv_w, conv_w, alog, dtb, norm_w.reshape(1, GDN_HEAD_DIM))


def _moba_prep_kernel(q_ref, k_ref, v_ref, qt_ref, k16_ref, vt_ref, km_ref):
    t = pl.program_id(1)
    qt_ref[...] = (q_ref[...] * (MOBA_HEAD_DIM ** -0.5)).T.astype(BF16)
    vt_ref[...] = v_ref[...].T.astype(BF16)
    k = k_ref[...]
    k16_ref[...] = k.astype(BF16)
    km_ref[pl.ds(t, 1), :] = jnp.mean(k, axis=0, keepdims=True)


def _moba_prep(proj, bsz, seq):
    blk = MOBA_BLOCK
    nb = seq // blk
    w = MOBA_WIDTH
    cb = COL_B // w
    return pl.pallas_call(
        _moba_prep_kernel,
        out_shape=(jax.ShapeDtypeStruct((bsz, w, seq), BF16),
                   jax.ShapeDtypeStruct((bsz * seq, w), BF16),
                   jax.ShapeDtypeStruct((bsz, w, seq), BF16),
                   jax.ShapeDtypeStruct((bsz, nb, w), F32)),
        grid=(bsz, nb),
        in_specs=[pl.BlockSpec((blk, w), lambda b, t: (b * nb + t, cb)),
                  pl.BlockSpec((blk, w), lambda b, t: (b * nb + t, cb + 1)),
                  pl.BlockSpec((blk, w), lambda b, t: (b * nb + t, cb + 2))],
        out_specs=(pl.BlockSpec((None, w, blk), lambda b, t: (b, 0, t)),
                   pl.BlockSpec((blk, w), lambda b, t: (b * nb + t, 0)),
                   pl.BlockSpec((None, w, blk), lambda b, t: (b, 0, t)),
                   pl.BlockSpec((None, nb, w), lambda b, t: (b, 0, 0))),
        compiler_params=_cparams(("parallel", "arbitrary")),
        name="moba_prep",
    )(proj, proj, proj)


def _moba_kernel(qi_ref, j0_ref, *refs):
    nkv = MOBA_KV_PER_STEP
    qt_ref = refs[0]
    k_refs = refs[1:1 + nkv]
    vt_refs = refs[1 + nkv:1 + 2 * nkv]
    km_ref, bown_ref, bprev_ref, bfar_ref, o_ref, m_ref, l_ref, acc_ref, msk_ref = refs[1 + 2 * nkv:]
    step = pl.program_id(1)
    qi = qi_ref[step]
    j0 = j0_ref[step]
    blk = MOBA_BLOCK
    hd = MOBA_HEAD_DIM
    nb = km_ref.shape[0]
    heads = range(MOBA_HEADS)

    def rows(h):
        return slice(h * hd, (h + 1) * hd)

    def mask_row(h, j):
        return msk_ref[h, pl.ds(jnp.where(j <= qi, qi - j, nb), 1), :]

    def update(bias):
        old = [(m_ref[h], l_ref[h], acc_ref[h]) for h in heads]
        new = []
        for h in heads:
            m_prev, l_prev, acc_prev = old[h]
            qt = qt_ref[rows(h), :]
            s = jnp.concatenate([_dot(k_refs[u][:, rows(h)], qt) + bias(h, u) for u in range(nkv)], axis=0)
            m_new = jnp.maximum(m_prev, jnp.max(s, axis=0, keepdims=True))
            alpha = jnp.exp(m_prev - m_new)
            p = jnp.exp(s - m_new)
            vt = jnp.concatenate([vt_refs[u][rows(h), :] for u in range(nkv)], axis=1)
            new.append((m_new, alpha * l_prev + jnp.sum(p, axis=0, keepdims=True),
                        alpha * acc_prev + _dot(vt, p.astype(BF16))))
        for h in heads:
            m_ref[h], l_ref[h], acc_ref[h] = new[h]

    @pl.when(j0 == 0)
    def _():
        blkid = lax.broadcasted_iota(jnp.int32, (nb, blk), 0)
        for h in heads:
            m_ref[h] = jnp.full((1, blk), NEG_INF, F32)
            l_ref[h] = jnp.zeros((1, blk), F32)
            acc_ref[h] = jnp.zeros((hd, blk), F32)
            k1, k2, _ = _split3(km_ref[:, rows(h)])
            qt = qt_ref[rows(h), :]
            gate = _dot(k1, qt) + _dot(k2, qt)
            gate = jnp.where(blkid < qi, gate, -jnp.inf)
            cnt = jnp.zeros((nb, blk), F32)
            for i in range(nb):
                gi = gate[i:i + 1, :]
                ahead = (gi > gate) | ((gi == gate) & (blkid > i))
                cnt = cnt + jnp.where(ahead, 1.0, 0.0)
            msk_ref[h, 0:nb, :] = jnp.where(cnt < MOBA_TOPK, 0.0, NEG_INF)
            msk_ref[h, nb:, :] = jnp.full((msk_ref.shape[1] - nb, blk), NEG_INF, F32)

        def bias(h, u):
            if u == 0:
                return bown_ref[h]
            return (bprev_ref[h] if u == 1 else bfar_ref[h]) + mask_row(h, u)

        update(bias)

    @pl.when(j0 > 0)
    def _():
        update(lambda h, u: bfar_ref[h] + mask_row(h, j0 + u))

    @pl.when(j0 + nkv > qi)
    def _():
        for h in heads:
            o_ref[:, rows(h)] = (acc_ref[h] / l_ref[h]).T.astype(o_ref.dtype)


def _rel_bucket(dist):
    n = jnp.maximum(dist, 0)
    max_exact = REL_BUCKETS // 2
    nf = jnp.maximum(n, max_exact).astype(F32)
    large = max_exact + (jnp.log(nf / max_exact) / math.log(REL_MAX_DIST / max_exact)
                         * (REL_BUCKETS - max_exact)).astype(jnp.int32)
    large = jnp.minimum(large, REL_BUCKETS - 1)
    return jnp.where(n < max_exact, n, large)


def _bias_lookup(rel_bias, dist):
    bucket = _rel_bucket(dist)[None]
    out = jnp.zeros((rel_bias.shape[1],) + dist.shape, F32)
    for b in range(REL_BUCKETS):
        out = jnp.where(bucket == b, rel_bias[b].astype(F32).reshape((-1,) + (1,) * dist.ndim), out)
    return out


def _moba(qt, k16, vt, kmean, rel_bias, bsz, seq):
    blk = MOBA_BLOCK
    nb = seq // blk
    w = MOBA_WIDTH
    assert blk >= REL_MAX_DIST
    kk = jnp.arange(blk)[:, None]
    qq = jnp.arange(blk)[None, :]
    b_own = jnp.where((qq >= kk)[None], _bias_lookup(rel_bias, qq - kk), NEG_INF)
    b_prev = _bias_lookup(rel_bias, blk + qq - kk)
    b_far = jnp.broadcast_to(_bias_lookup(rel_bias, jnp.full((1, 1), 2 * blk)), (MOBA_HEADS, 1, blk))
    nkv = MOBA_KV_PER_STEP
    steps = [(qi, j0) for qi in range(nb) for j0 in range(0, qi + 1, nkv)]
    qi_arr = jnp.asarray(np.array([s[0] for s in steps], np.int32))
    j0_arr = jnp.asarray(np.array([s[1] for s in steps], np.int32))

    def kblk(u):
        return lambda b, s, qi, j0: (b * nb + jnp.maximum(qi[s] - j0[s] - u, 0), 0)

    def vblk(u):
        return lambda b, s, qi, j0: (b, 0, jnp.maximum(qi[s] - j0[s] - u, 0))

    grid_spec = pltpu.PrefetchScalarGridSpec(
        num_scalar_prefetch=2,
        grid=(bsz, len(steps)),
        in_specs=(
            [pl.BlockSpec((None, w, blk), lambda b, s, qi, j0: (b, 0, qi[s]))]
            + [pl.BlockSpec((blk, w), kblk(u)) for u in range(nkv)]
            + [pl.BlockSpec((None, w, blk), vblk(u)) for u in range(nkv)]
            + [pl.BlockSpec((None, nb, w), lambda b, s, qi, j0: (b, 0, 0)),
               pl.BlockSpec((MOBA_HEADS, blk, blk), lambda b, s, qi, j0: (0, 0, 0)),
               pl.BlockSpec((MOBA_HEADS, blk, blk), lambda b, s, qi, j0: (0, 0, 0)),
               pl.BlockSpec((MOBA_HEADS, 1, blk), lambda b, s, qi, j0: (0, 0, 0))]),
        out_specs=pl.BlockSpec((blk, w), lambda b, s, qi, j0: (b * nb + qi[s], 0)),
        scratch_shapes=[
            pltpu.VMEM((MOBA_HEADS, 1, blk), F32),
            pltpu.VMEM((MOBA_HEADS, 1, blk), F32),
            pltpu.VMEM((MOBA_HEADS, MOBA_HEAD_DIM, blk), F32),
            pltpu.VMEM((MOBA_HEADS, nb + MASK_PAD_ROWS, blk), F32),
        ],
    )
    return pl.pallas_call(
        _moba_kernel,
        out_shape=jax.ShapeDtypeStruct((bsz * seq, w), BF16),
        grid_spec=grid_spec,
        compiler_params=_cparams(("parallel", "arbitrary")),
        name="moba",
    )(qi_arr, j0_arr, qt, *([k16] * nkv), *([vt] * nkv), kmean, b_own, b_prev, b_far)


def _sconv_kernel(x_ref, bg_ref, cg_ref, w_ref, o_ref, ext_ref):
    seq = x_ref.shape[0]
    u = cg_ref[...] * x_ref[...]
    ext_ref[0:CONV_HALO, :] = jnp.zeros((CONV_HALO, u.shape[1]), F32)
    ext_ref[CONV_HALO:, :] = u
    width = w_ref.shape[0]
    y = u * w_ref[width - 1:width, :]
    for s in range(1, width):
        y = y + ext_ref[CONV_HALO - s:CONV_HALO - s + seq, :] * w_ref[width - 1 - s:width - s, :]
    o_ref[...] = (bg_ref[...] * y).astype(o_ref.dtype)


def _sconv(proj, conv_w, bsz, seq):
    cw = SCONV_WIDTH
    nc = cw // LANES
    c0 = COL_C // LANES
    return pl.pallas_call(
        _sconv_kernel,
        out_shape=jax.ShapeDtypeStruct((bsz * seq, cw), BF16),
        grid=(bsz, nc),
        in_specs=[
            pl.BlockSpec((seq, LANES), lambda b, c: (b, c0 + c)),
            pl.BlockSpec((seq, LANES), lambda b, c: (b, c0 + nc + c)),
            pl.BlockSpec((seq, LANES), lambda b, c: (b, c0 + 2 * nc + c)),
            pl.BlockSpec((SCONV_K, LANES), lambda b, c: (0, c)),
        ],
        out_specs=pl.BlockSpec((seq, LANES), lambda b, c: (b, c)),
        scratch_shapes=[pltpu.VMEM((CONV_HALO + seq, LANES), F32)],
        compiler_params=_cparams(("parallel", "parallel")),
        name="sconv",
    )(proj, proj, proj, conv_w)


def _out_kernel(x_ref, oa_ref, ob_ref, oc_ref, w_ref, o_ref):
    a0, a1, a2 = GDN_WIDTH, GDN_WIDTH + MOBA_WIDTH, D_MODEL
    y = _dot(oa_ref[...], w_ref[0:a0, :])
    y = y + _dot(ob_ref[...], w_ref[a0:a1, :])
    y = y + _dot(oc_ref[...], w_ref[a1:a2, :])
    o_ref[...] = x_ref[...] + y


def _out(x, l, oa, ob, oc, w):
    n, d = x.shape
    return pl.pallas_call(
        _out_kernel,
        out_shape=jax.ShapeDtypeStruct((n, d), F32),
        grid=(n // TM_OUT,),
        in_specs=[
            pl.BlockSpec((TM_OUT, d), lambda i: (i, 0)),
            pl.BlockSpec((TM_OUT, GDN_WIDTH), lambda i: (i, 0)),
            pl.BlockSpec((TM_OUT, MOBA_WIDTH), lambda i: (i, 0)),
            pl.BlockSpec((TM_OUT, SCONV_WIDTH), lambda i: (i, 0)),
            pl.BlockSpec((None, d, d), lambda i: (l, 0, 0)),
        ],
        out_specs=pl.BlockSpec((TM_OUT, d), lambda i: (i, 0)),
        compiler_params=_cparams(("parallel",)),
        name="out_proj",
    )(x, oa, ob, oc, w)


def _mixer(x, l, mix_norm, w_in, w_in16, w_bc16, conv_qkv_w, a_log, dt_bias, gdn_norm_w, conv_c_w, w_out, rel_bias,
           bsz, seq):
    proj, ba = _proj(x, l, mix_norm, w_in, w_in16, w_bc16)
    o_a = _gdn(proj, ba, conv_qkv_w, a_log, dt_bias, gdn_norm_w, bsz, seq)
    o_b = _moba(*_moba_prep(proj, bsz, seq), rel_bias, bsz, seq)
    o_c = _sconv(proj, conv_c_w, bsz, seq)
    return _out(x, l, o_a, o_b, o_c, w_out)


def kernel(x, ffn1_norm, ffn1_w_gate, ffn1_w_up, ffn1_w_down, mix_norm, w_in, conv_qkv_w, a_log, dt_bias,
           gdn_norm_w, conv_c_w, w_out, ffn2_norm, ffn2_w_gate, ffn2_w_up, ffn2_w_down, rel_bias, final_norm):
    bsz, seq, d = x.shape
    x = x.reshape(bsz * seq, d)
    ffn1 = [w.astype(BF16) for w in (ffn1_w_gate, ffn1_w_up, ffn1_w_down)]
    ffn2 = [w.astype(BF16) for w in (ffn2_w_gate, ffn2_w_up, ffn2_w_down)]
    w_in = jnp.swapaxes(w_in, 1, 2)
    w_in16 = w_in.astype(BF16)
    w_bc16 = w_in16[:, COL_B + 2 * GDN_HEADS:, :]
    w_out16 = w_out.astype(BF16)
    for l in range(DEPTH):
        x = _ffn(x, l, ffn1_norm[l], *ffn1, final_norm, False)
        x = _mixer(x, l, mix_norm[l], w_in, w_in16, w_bc16, conv_qkv_w[l], a_log[l], dt_bias[l], gdn_norm_w[l],
                   conv_c_w[l], w_out16, rel_bias, bsz, seq)
        x = _ffn(x, l, ffn2_norm[l], *ffn2, final_norm, l == DEPTH - 1)
    return x.reshape(bsz, seq, d)
```
